```python
import math
import jax, jax.numpy as jnp
from jax import lax
import numpy as np

D_MODEL = 2048
BATCH = 1
SEQ = 8192
DEPTH = 4

N_MIXERS = 3
DIFF_HEADS = 8
DIFF_HEAD_DIM = D_MODEL // DIFF_HEADS // 2
DIFF_QBLOCK = 128
POOL_WINDOWS = (2, 4, 8, 16)
POOL_GROUP = D_MODEL // len(POOL_WINDOWS)
SWA_HEAD_DIM = 64
SWA_Q_HEADS = D_MODEL // SWA_HEAD_DIM
SWA_KV_HEADS = SWA_Q_HEADS // 8
SWA_WINDOW = 128
D_FF = 7 * D_MODEL // 2
N_EXPERTS = 8
TOP_K = 2
MOE_BLOCK = 256
ROPE_THETA = 10000.0
NORM_EPS = 1e-6
SUBLN_EPS = 1e-5
MASK_VALUE = -1e30
N_A = (DEPTH + 2) // 3
N_B = (DEPTH + 1) // 3
N_C = DEPTH // 3
N_DENSE = (DEPTH + 1) // 2
N_MOE = DEPTH // 2

kernel_name = 'hybrid_diffattn_pool_swa_moe'


def rms_norm(x, g, eps=NORM_EPS):
    xf = x.astype(jnp.float32)
    y = xf * lax.rsqrt(jnp.mean(xf * xf, axis=-1, keepdims=True) + eps)
    return (y * g.astype(jnp.float32)).astype(x.dtype)


def rope(x, positions):
    dh = x.shape[-1]
    half = dh // 2
    inv = ROPE_THETA ** (-jnp.arange(half, dtype=jnp.float32) * 2.0 / dh)
    ang = positions.astype(jnp.float32)[:, :, None] * inv
    cos = jnp.cos(ang)[:, :, None, :]
    sin = jnp.sin(ang)[:, :, None, :]
    xf = x.astype(jnp.float32)
    x1, x2 = xf[..., :half], xf[..., half:]
    return jnp.concatenate([x1 * cos - x2 * sin, x2 * cos + x1 * sin], axis=-1).astype(x.dtype)


def diff_attention(h, positions, w_qkv, lq1, lk1, lq2, lk2, subln, w_o, lambda_init):
    B, S, D = h.shape
    H, dh = DIFF_HEADS, DIFF_HEAD_DIM
    q, k, v = jnp.split(h @ w_qkv, 3, axis=-1)
    q = rope(q.reshape(B, S, 2 * H, dh), positions) * (dh ** -0.5)
    k = rope(k.reshape(B, S, 2 * H, dh), positions)
    v = v.reshape(B, S, H, 2 * dh)
    lam = (jnp.exp(jnp.sum(lq1.astype(jnp.float32) * lk1.astype(jnp.float32)))
           - jnp.exp(jnp.sum(lq2.astype(jnp.float32) * lk2.astype(jnp.float32)))
           + lambda_init)
    k_t = k.transpose(0, 2, 1, 3)
    v_t = v.transpose(0, 2, 1, 3)
    nb = S // DIFF_QBLOCK
    q_blocks = q.reshape(B, nb, DIFF_QBLOCK, 2 * H, dh).transpose(1, 0, 3, 2, 4)
    key_pos = jnp.arange(S)

    def block(args):
        q_blk, n = args
        s = jnp.einsum('bhqd,bhkd->bhqk', q_blk, k_t).astype(jnp.float32)
        q_pos = n * DIFF_QBLOCK + jnp.arange(DIFF_QBLOCK)
        s = jnp.where(key_pos[None, :] <= q_pos[:, None], s, MASK_VALUE)
        p = jax.nn.softmax(s, axis=-1).reshape(B, H, 2, DIFF_QBLOCK, S)
        a = p[:, :, 0] - lam * p[:, :, 1]
        return jnp.einsum('bhqk,bhkd->bhqd', a.astype(v_t.dtype), v_t)

    o = lax.map(block, (q_blocks, jnp.arange(nb)))
    o = o.transpose(1, 0, 3, 2, 4).reshape(B, S, H, 2 * dh)
    o = rms_norm(o, subln, SUBLN_EPS) * (1.0 - lambda_init)
    return o.reshape(B, S, D) @ w_o


def pool_mixer(h, w_pool, scale):
    B, S, D = h.shape
    xf = h.astype(jnp.float32)
    t = jnp.arange(S)
    groups = []
    for g, w in enumerate(POOL_WINDOWS):
        xg = xf[..., g * POOL_GROUP:(g + 1) * POOL_GROUP]
        cs = jnp.cumsum(xg, axis=1)
        cs_lag = jnp.pad(cs[:, :S - w], ((0, 0), (w, 0), (0, 0)))
        cnt = jnp.minimum(t + 1, w).astype(jnp.float32)[None, :, None]
        groups.append((cs - cs_lag) / cnt - xg)
    pooled = jnp.stack(groups, axis=2).astype(h.dtype)
    y = jnp.einsum('bsgc,gce->bsge', pooled, w_pool).reshape(B, S, D)
    return y * scale


def swa_attention(h, positions, w_qkv, b_qkv, sinks, w_o, b_o):
    B, S, D = h.shape
    Hq, Hkv, dh, W = SWA_Q_HEADS, SWA_KV_HEADS, SWA_HEAD_DIM, SWA_WINDOW
    G = Hq // Hkv
    qkv = h @ w_qkv + b_qkv
    q = qkv[..., :Hq * dh].reshape(B, S, Hq, dh)
    k = qkv[..., Hq * dh:(Hq + Hkv) * dh].reshape(B, S, Hkv, dh)
    v = qkv[..., (Hq + Hkv) * dh:].reshape(B, S, Hkv, dh)
    q = rope(q, positions) * (dh ** -0.5)
    k = rope(k, positions)
    nb = S // W
    q_b = q.reshape(B, nb, W, Hkv, G, dh)

    def band(t):
        t = t.reshape(B, nb, W, Hkv, dh)
        prev = jnp.pad(t[:, :-1], ((0, 0), (1, 0), (0, 0), (0, 0), (0, 0)))
        return jnp.concatenate([prev, t], axis=2)

    k_band, v_band = band(k), band(v)
    s = jnp.einsum('bnqhgd,bnkhd->bnhgqk', q_b, k_band).astype(jnp.float32)
    blk = jnp.arange(nb)[:, None, None]
    q_pos = blk * W + jnp.arange(W)[None, :, None]
    k_pos = (blk - 1) * W + jnp.arange(2 * W)[None, None, :]
    valid = (k_pos >= 0) & (k_pos <= q_pos) & (q_pos - k_pos < W)
    s = jnp.where(valid[None, :, None, None], s, MASK_VALUE)
    sink = sinks.astype(jnp.float32).reshape(Hkv, G)[None, None, :, :, None]
    lse = jnp.logaddexp(jax.nn.logsumexp(s, axis=-1), sink)
    p = jnp.exp(s - lse[..., None])
    o = jnp.einsum('bnhgqk,bnkhd->bnqhgd', p.astype(v_band.dtype), v_band).reshape(B, S, Hq * dh)
    return o @ w_o + b_o


def swiglu(h, w_gate, w_up, w_down):
    return (jax.nn.silu(h @ w_gate) * (h @ w_up)) @ w_down


def moe_ffn(h, w_router, we_gate, we_up, we_down):
    B, S, D = h.shape
    T = B * S
    xt = h.reshape(T, D)
    logits = (xt @ w_router).astype(jnp.float32)
    top_val, top_idx = lax.top_k(logits, TOP_K)
    gates = jax.nn.softmax(top_val, axis=-1)
    e_flat = top_idx.reshape(-1).astype(jnp.int32)
    tok_flat = jnp.repeat(jnp.arange(T, dtype=jnp.int32), TOP_K)
    g_flat = gates.reshape(-1)
    order = jnp.argsort(e_flat)
    e_sorted, tok_sorted, g_sorted = e_flat[order], tok_flat[order], g_flat[order]
    sizes = jnp.bincount(e_flat, length=N_EXPERTS).astype(jnp.int32)
    starts = jnp.cumsum(sizes) - sizes
    padded = (sizes + MOE_BLOCK - 1) // MOE_BLOCK * MOE_BLOCK
    p_ends = jnp.cumsum(padded)
    p_starts = p_ends - padded
    dest = p_starts[e_sorted] + jnp.arange(T * TOP_K, dtype=jnp.int32) - starts[e_sorted]
    n_blocks = -(-(T * TOP_K) // MOE_BLOCK) + N_EXPERTS
    n_rows = n_blocks * MOE_BLOCK
    row_tok = jnp.full((n_rows,), T, jnp.int32).at[dest].set(tok_sorted)
    row_gate = jnp.zeros((n_rows,), jnp.float32).at[dest].set(g_sorted)
    block_e = jnp.minimum(jnp.searchsorted(p_ends, jnp.arange(n_blocks, dtype=jnp.int32) * MOE_BLOCK, side='right'), N_EXPERTS - 1)
    x_rows = xt[jnp.minimum(row_tok, T - 1)].reshape(n_blocks, MOE_BLOCK, D)

    def expert_block(args):
        xb, e = args
        return swiglu(xb, we_gate[e], we_up[e], we_down[e])

    y_rows = lax.map(expert_block, (x_rows, block_e)).reshape(n_rows, D)
    out = jnp.zeros((T, D), jnp.float32).at[row_tok].add(y_rows.astype(jnp.float32) * row_gate[:, None], mode='drop')
    return out.astype(h.dtype).reshape(B, S, D)


def setup_inputs(seed: int = 0) -> dict:
    key = jax.random.key(seed)
    ks = iter(jax.random.split(key, 32))

    def nrm(shape, scale):
        return jax.random.normal(next(ks), shape, jnp.float32) * scale

    def gain(shape):
        return 1.0 + nrm(shape, 0.02)

    D = D_MODEL
    dh = DIFF_HEAD_DIM
    swa_in = (SWA_Q_HEADS + 2 * SWA_KV_HEADS) * SWA_HEAD_DIM
    swa_out = SWA_Q_HEADS * SWA_HEAD_DIM
    x = nrm((BATCH, SEQ, D), 1.0)
    positions = (jax.random.randint(next(ks), (BATCH, 1), 0, 4096, dtype=jnp.int32)
                 + jnp.arange(SEQ, dtype=jnp.int32)[None, :])
    return {
        'x': x,
        'positions': positions,
        'norm1': gain((DEPTH, D)),
        'norm2': gain((DEPTH, D)),
        'final_norm': gain((D,)),
        'diff_w_qkv': nrm((N_A, D, 3 * D), D ** -0.5),
        'diff_lambda_q1': nrm((N_A, dh), 0.1),
        'diff_lambda_k1': nrm((N_A, dh), 0.1),
        'diff_lambda_q2': nrm((N_A, dh), 0.1),
        'diff_lambda_k2': nrm((N_A, dh), 0.1),
        'diff_subln': gain((N_A, 2 * dh)),
        'diff_w_o': nrm((N_A, D, D), D ** -0.5),
        'pool_w': nrm((N_B, len(POOL_WINDOWS), POOL_GROUP, POOL_GROUP), POOL_GROUP ** -0.5),
        'pool_scale': 1.0 + nrm((N_B, D), 0.1),
        'swa_w_qkv': nrm((N_C, D, swa_in), D ** -0.5),
        'swa_b_qkv': nrm((N_C, swa_in), 0.01),
        'swa_sinks': nrm((N_C, SWA_Q_HEADS), 1.0),
        'swa_w_o': nrm((N_C, swa_out, D), swa_out ** -0.5),
        'swa_b_o': nrm((N_C, D), 0.01),
        'ffn_w_gate': nrm((N_DENSE, D, D_FF), D ** -0.5),
        'ffn_w_up': nrm((N_DENSE, D, D_FF), D ** -0.5),
        'ffn_w_down': nrm((N_DENSE, D_FF, D), D_FF ** -0.5),
        'moe_router': nrm((N_MOE, D, N_EXPERTS), D ** -0.5),
        'moe_w_gate': nrm((N_MOE, N_EXPERTS, D, D_FF), D ** -0.5),
        'moe_w_up': nrm((N_MOE, N_EXPERTS, D, D_FF), D ** -0.5),
        'moe_w_down': nrm((N_MOE, N_EXPERTS, D_FF, D), D_FF ** -0.5),
    }


def reference(x, positions, norm1, norm2, final_norm,
              diff_w_qkv, diff_lambda_q1, diff_lambda_k1, diff_lambda_q2, diff_lambda_k2, diff_subln, diff_w_o,
              pool_w, pool_scale,
              swa_w_qkv, swa_b_qkv, swa_sinks, swa_w_o, swa_b_o,
              ffn_w_gate, ffn_w_up, ffn_w_down,
              moe_router, moe_w_gate, moe_w_up, moe_w_down):
    for i in range(DEPTH):
        h = rms_norm(x, norm1[i])
        kind, j = i % N_MIXERS, i // N_MIXERS
        if kind == 0:
            lambda_init = 0.8 - 0.6 * math.exp(-0.3 * i)
            y = diff_attention(h, positions, diff_w_qkv[j], diff_lambda_q1[j], diff_lambda_k1[j],
                               diff_lambda_q2[j], diff_lambda_k2[j], diff_subln[j], diff_w_o[j], lambda_init)
        elif kind == 1:
            y = pool_mixer(h, pool_w[j], pool_scale[j])
        else:
            y = swa_attention(h, positions, swa_w_qkv[j], swa_b_qkv[j], swa_sinks[j], swa_w_o[j], swa_b_o[j])
        x = x + y
        h = rms_norm(x, norm2[i])
        m = i // 2
        if i % 2 == 0:
            y = swiglu(h, ffn_w_gate[m], ffn_w_up[m], ffn_w_down[m])
        else:
            y = moe_ffn(h, moe_router[m], moe_w_gate[m], moe_w_up[m], moe_w_down[m])
        x = x + y
    return rms_norm(x, final_norm)
```

```python
import functools
import math

import numpy as np
import jax
import jax.numpy as jnp
from jax import lax
from jax.experimental import pallas as pl
from jax.experimental.pallas import tpu as pltpu

F32 = jnp.float32
BF16 = jnp.bfloat16

D_MODEL = 2048
N_LAYERS = 4
NORM_EPS = 1e-6
SUBLN_EPS = 1e-5
MASK_VALUE = -1e30
ROPE_THETA = 10000.0

DIFF_HEADS = 8
DIFF_DH = 128
SWA_DH = 64
SWA_Q_HEADS = 32
SWA_KV_HEADS = 4
SWA_WINDOW = 128
POOL_WINDOWS = (2, 4, 8, 16)
POOL_GROUP = 512
POOL_HALO = 16
N_EXPERTS = 8

LANES = 128
VMEM_LIMIT = 56 * 1024 * 1024

MM_TM = 1024
MM_TN = 512
NORM_TM = 512
ATT_TQ = 512
SWA_TQ = 256
POOL_TM = 512
FFN_SUB = 256
FFN_NSUB = 9
FFN_TF = 256
ROUTER_TM = 256
SCATTER_CT = 512
COMBINE_CT = 256


def _params(*sem):
    return pltpu.CompilerParams(dimension_semantics=sem, vmem_limit_bytes=VMEM_LIMIT)


def _rms(x, g, eps):
    return x * lax.rsqrt(jnp.mean(x * x, axis=-1, keepdims=True) + eps) * g


def _rope_table_kernel(pos_ref, inv_ref, sgn_ref, cos_ref, sin_ref):
    ang = pos_ref[...].astype(F32) * inv_ref[...]
    cos_ref[...] = jnp.cos(ang)
    sin_ref[...] = jnp.sin(ang) * sgn_ref[...]


def _rope_tables(pos_col, dh):
    s = pos_col.shape[0]
    half = dh // 2
    lane = np.arange(LANES)
    inv = (ROPE_THETA ** (-(lane % half).astype(np.float32) * 2.0 / dh)).astype(np.float32)
    sgn = np.where((lane % dh) < half, -1.0, 1.0).astype(np.float32)
    tm = 1024
    return pl.pallas_call(
        _rope_table_kernel,
        grid=(s // tm,),
        in_specs=[pl.BlockSpec((tm, 1), lambda i: (i, 0)),
                  pl.BlockSpec((1, LANES), lambda i: (0, 0)),
                  pl.BlockSpec((1, LANES), lambda i: (0, 0))],
        out_specs=[pl.BlockSpec((tm, LANES), lambda i: (i, 0))] * 2,
        out_shape=[jax.ShapeDtypeStruct((s, LANES), F32)] * 2,
        compiler_params=_params("arbitrary"),
        name="rope_tables",
    )(pos_col, jnp.asarray(inv)[None, :], jnp.asarray(sgn)[None, :])


def _rope_apply(x, cos, sin_signed, half):
    tn = x.shape[1]
    reps = tn // LANES
    c = jnp.concatenate([cos] * reps, axis=1)
    s = jnp.concatenate([sin_signed] * reps, axis=1)
    lane = lax.broadcasted_iota(jnp.int32, x.shape, 1)
    first = (lane % (2 * half)) < half
    swapped = jnp.where(first, pltpu.roll(x, tn - half, 1), pltpu.roll(x, half, 1))
    return x * c + swapped * s


def _norm_kernel(x_ref, g_ref, o_ref):
    o_ref[...] = _rms(x_ref[...], g_ref[...], NORM_EPS).astype(o_ref.dtype)


def _norm(x, gains, layer, out_dtype):
    t, d = x.shape
    g3 = gains.reshape(gains.shape[0], 1, d)
    return pl.pallas_call(
        _norm_kernel,
        grid=(t // NORM_TM,),
        in_specs=[pl.BlockSpec((NORM_TM, d), lambda i: (i, 0)),
                  pl.BlockSpec((None, 1, d), lambda i: (layer, 0, 0))],
        out_specs=pl.BlockSpec((NORM_TM, d), lambda i: (i, 0)),
        out_shape=jax.ShapeDtypeStruct((t, d), out_dtype),
        compiler_params=_params("arbitrary"),
        name="rmsnorm",
    )(x, g3)


def _mm_qkv_kernel(a_ref, w_ref, b_ref, cos_ref, sin_ref, o_ref, wb_ref, *,
                   half, scale, q_tiles, k_cols):
    jn = pl.program_id(0)

    @pl.when(pl.program_id(1) == 0)
    def _():
        wb_ref[...] = w_ref[...].astype(BF16)

    acc = jnp.dot(a_ref[...], wb_ref[...], preferred_element_type=F32) + b_ref[...]
    roped = _rope_apply(acc, cos_ref[...], sin_ref[...], half)

    @pl.when(jn < q_tiles)
    def _():
        o_ref[...] = (roped * scale).astype(o_ref.dtype)

    @pl.when(jn >= q_tiles)
    def _():
        rel = (jn - q_tiles) * acc.shape[1] + lax.broadcasted_iota(jnp.int32, acc.shape, 1)
        o_ref[...] = jnp.where(rel < k_cols, roped, acc).astype(o_ref.dtype)


def _mm_qkv(a, w, layer, bias, cos, sin, *, half, scale, q_cols, k_cols):
    m, k = a.shape
    n = w.shape[2]
    tm, tn = MM_TM, MM_TN
    assert q_cols % tn == 0
    kern = functools.partial(_mm_qkv_kernel, half=half, scale=scale, q_tiles=q_cols // tn,
                             k_cols=k_cols)
    return pl.pallas_call(
        kern,
        grid=(n // tn, m // tm),
        in_specs=[pl.BlockSpec((tm, k), lambda j, i: (i, 0)),
                  pl.BlockSpec((None, k, tn), lambda j, i: (layer, 0, j)),
                  pl.BlockSpec((1, tn), lambda j, i: (0, j)),
                  pl.BlockSpec((tm, LANES), lambda j, i: (i, 0)),
                  pl.BlockSpec((tm, LANES), lambda j, i: (i, 0))],
        out_specs=pl.BlockSpec((tm, tn), lambda j, i: (i, j)),
        out_shape=jax.ShapeDtypeStruct((m, n), BF16),
        scratch_shapes=[pltpu.VMEM((k, tn), BF16)],
        compiler_params=_params("arbitrary", "arbitrary"),
        name="qkv_proj",
    )(a, w, bias, cos, sin)


def _mm_res_kernel(a_ref, w_ref, b_ref, x_ref, o_ref, wb_ref):
    @pl.when(pl.program_id(1) == 0)
    def _():
        wb_ref[...] = w_ref[...].astype(BF16)

    acc = jnp.dot(a_ref[...], wb_ref[...], preferred_element_type=F32)
    o_ref[...] = x_ref[...] + (acc + b_ref[...])


def _mm_residual(a, w, layer, bias, x):
    m, k = a.shape
    n = w.shape[2]
    tm, tn = MM_TM, MM_TN
    return pl.pallas_call(
        _mm_res_kernel,
        grid=(n // tn, m // tm),
        in_specs=[pl.BlockSpec((tm, k), lambda j, i: (i, 0)),
                  pl.BlockSpec((None, k, tn), lambda j, i: (layer, 0, j)),
                  pl.BlockSpec((1, tn), lambda j, i: (0, j)),
                  pl.BlockSpec((tm, tn), lambda j, i: (i, j))],
        out_specs=pl.BlockSpec((tm, tn), lambda j, i: (i, j)),
        out_shape=jax.ShapeDtypeStruct((m, n), F32),
        scratch_shapes=[pltpu.VMEM((k, tn), BF16)],
        compiler_params=_params("arbitrary", "arbitrary"),
        name="out_proj",
    )(a, w, bias, x)


def _diff_attn_kernel(lq1_ref, lk1_ref, lq2_ref, lk2_ref, sub_ref,
                      q1_ref, q2_ref, k1_ref, k2_ref, v_ref, o_ref,
                      m1_ref, l1_ref, a1_ref, m2_ref, l2_ref, a2_ref, *, tq, lambda_init):
    i = pl.program_id(1)
    for m_ref, l_ref, a_ref in ((m1_ref, l1_ref, a1_ref), (m2_ref, l2_ref, a2_ref)):
        m_ref[...] = jnp.full(m_ref.shape, MASK_VALUE, F32)
        l_ref[...] = jnp.zeros(l_ref.shape, F32)
        a_ref[...] = jnp.zeros(a_ref.shape, F32)

    def step(j, masked):
        ks = pl.ds(pl.multiple_of(j * tq, tq), tq)
        vblk = v_ref[ks, :]
        for q_ref, k_ref, m_ref, l_ref, a_ref in ((q1_ref, k1_ref, m1_ref, l1_ref, a1_ref),
                                                  (q2_ref, k2_ref, m2_ref, l2_ref, a2_ref)):
            s = lax.dot_general(q_ref[...], k_ref[ks, :], (((1,), (1,)), ((), ())),
                                preferred_element_type=F32)
            if masked:
                row = lax.broadcasted_iota(jnp.int32, s.shape, 0)
                col = lax.broadcasted_iota(jnp.int32, s.shape, 1)
                s = jnp.where(col <= row, s, MASK_VALUE)
            m_old = m_ref[...]
            m_new = jnp.maximum(m_old, jnp.max(s, axis=-1, keepdims=True))
            p = jnp.exp(s - m_new)
            alpha = jnp.exp(m_old - m_new)
            l_ref[...] = alpha * l_ref[...] + jnp.sum(p, axis=-1, keepdims=True)
            a_ref[...] = alpha * a_ref[...] + jnp.dot(p.astype(BF16), vblk,
                                                      preferred_element_type=F32)
            m_ref[...] = m_new

    def full_step(j, carry):
        step(j, False)
        return carry

    lax.fori_loop(0, i, full_step, 0)
    step(i, True)

    lam = (jnp.exp(jnp.sum(lq1_ref[...] * lk1_ref[...], axis=-1, keepdims=True))
           - jnp.exp(jnp.sum(lq2_ref[...] * lk2_ref[...], axis=-1, keepdims=True))
           + lambda_init)
    o = a1_ref[...] / l1_ref[...] - lam * (a2_ref[...] / l2_ref[...])
    o = _rms(o, sub_ref[...], SUBLN_EPS) * (1.0 - lambda_init)
    o_ref[...] = o.astype(o_ref.dtype)


def _diff_attention(qkv, lq1, lk1, lq2, lk2, subln, layer, lambda_init):
    s = qkv.shape[0]
    tq = ATT_TQ
    dh, dv = DIFF_DH, 2 * DIFF_DH
    k_off = (2 * DIFF_HEADS * dh) // dh
    v_off = (4 * DIFF_HEADS * dh) // dv
    vec = lambda a: a.reshape(a.shape[0], 1, a.shape[1])
    lam_spec = pl.BlockSpec((None, 1, dh), lambda h, i: (layer, 0, 0))
    kern = functools.partial(_diff_attn_kernel, tq=tq, lambda_init=lambda_init)
    return pl.pallas_call(
        kern,
        grid=(DIFF_HEADS, s // tq),
        in_specs=[lam_spec, lam_spec, lam_spec, lam_spec,
                  pl.BlockSpec((None, 1, dv), lambda h, i: (layer, 0, 0)),
                  pl.BlockSpec((tq, dh), lambda h, i: (i, 2 * h)),
                  pl.BlockSpec((tq, dh), lambda h, i: (i, 2 * h + 1)),
                  pl.BlockSpec((s, dh), lambda h, i: (0, k_off + 2 * h)),
                  pl.BlockSpec((s, dh), lambda h, i: (0, k_off + 2 * h + 1)),
                  pl.BlockSpec((s, dv), lambda h, i: (0, v_off + h))],
        out_specs=pl.BlockSpec((tq, dv), lambda h, i: (i, h)),
        out_shape=jax.ShapeDtypeStruct((s, DIFF_HEADS * dv), BF16),
        scratch_shapes=[pltpu.VMEM((tq, 1), F32), pltpu.VMEM((tq, 1), F32), pltpu.VMEM((tq, dv), F32),
                        pltpu.VMEM((tq, 1), F32), pltpu.VMEM((tq, 1), F32), pltpu.VMEM((tq, dv), F32)],
        compiler_params=_params("arbitrary", "arbitrary"),
        name="diff_attention",
    )(vec(lq1), vec(lk1), vec(lq2), vec(lk2), vec(subln), qkv, qkv, qkv, qkv, qkv)


def _swa_attn_kernel(sinks_ref, q_ref, kv_ref, kvp_ref, o_ref, *, tq):
    w = SWA_WINDOW
    dh = SWA_DH
    pairs = SWA_Q_HEADS // SWA_KV_HEADS // 2
    tile = pl.program_id(0)
    kcols = SWA_KV_HEADS * dh

    row = lax.broadcasted_iota(jnp.int32, (pairs * w, 2 * w), 0) % w
    col = lax.broadcasted_iota(jnp.int32, (pairs * w, 2 * w), 1)
    band = (col > row) & (col <= row + w)
    lane = lax.broadcasted_iota(jnp.int32, (2 * w, LANES), 1)
    zeros = jnp.zeros((2 * w, LANES), BF16)

    for b in range(tq // w):
        if b == 0:
            prev, first_col = kvp_ref[...], jnp.where(tile > 0, 0, w)
        else:
            prev, first_col = kv_ref[(b - 1) * w:b * w, :], 0
        kvb = jnp.concatenate([prev, kv_ref[b * w:(b + 1) * w, :]], axis=0)
        valid = band & (col >= first_col)
        for hk in range(SWA_KV_HEADS):
            c0 = (hk // 2) * LANES
            kc = kvb[:, c0:c0 + LANES]
            vc = kvb[:, kcols + c0:kcols + c0 + LANES]
            own = (lane // dh) == (hk % 2)
            k_own = jnp.where(own, kc, zeros)
            v_own = jnp.where(own, vc, zeros)
            k_sw = pltpu.roll(k_own, dh, 1)
            v_sw = pltpu.roll(v_own, dh, 1)
            if hk % 2 == 0:
                k_even, k_odd, v_even, v_odd = k_own, k_sw, v_own, v_sw
            else:
                k_even, k_odd, v_even, v_odd = k_sw, k_own, v_sw, v_own
            kbd = jnp.concatenate([k_even, k_odd], axis=0)
            vbd = jnp.concatenate([v_even, v_odd], axis=0)
            qp = jnp.concatenate(
                [q_ref[b * w:(b + 1) * w, (hk * pairs + p) * LANES:(hk * pairs + p + 1) * LANES]
                 for p in range(pairs)], axis=0)
            s = lax.dot_general(qp, kbd, (((1,), (1,)), ((), ())), preferred_element_type=F32)
            ps = []
            for par in range(2):
                sh = jnp.where(valid, s[:, par * 2 * w:(par + 1) * 2 * w], MASK_VALUE)
                sink = jnp.concatenate(
                    [jnp.full((w, 1), sinks_ref[hk * 2 * pairs + 2 * p + par], F32)
                     for p in range(pairs)], axis=0)
                m = jnp.maximum(jnp.max(sh, axis=-1, keepdims=True), sink)
                e = jnp.exp(sh - m)
                denom = jnp.sum(e, axis=-1, keepdims=True) + jnp.exp(sink - m)
                ps.append((e / denom).astype(BF16))
            o = jnp.dot(jnp.concatenate(ps, axis=1), vbd, preferred_element_type=F32)
            for p in range(pairs):
                o_ref[b * w:(b + 1) * w, (hk * pairs + p) * LANES:(hk * pairs + p + 1) * LANES] = \
                    o[p * w:(p + 1) * w, :].astype(o_ref.dtype)


def _swa_attention(qkv, sinks):
    s = qkv.shape[0]
    tq = SWA_TQ
    qcols = SWA_Q_HEADS * SWA_DH
    kvcols = 2 * SWA_KV_HEADS * SWA_DH
    kern = functools.partial(_swa_attn_kernel, tq=tq)
    return pl.pallas_call(
        kern,
        grid=(s // tq,),
        in_specs=[pl.BlockSpec(memory_space=pltpu.SMEM),
                  pl.BlockSpec((tq, qcols), lambda i: (i, 0)),
                  pl.BlockSpec((tq, kvcols), lambda i: (i, qcols // kvcols)),
                  pl.BlockSpec((SWA_WINDOW, kvcols),
                               lambda i: (jnp.maximum(i * (tq // SWA_WINDOW) - 1, 0), qcols // kvcols))],
        out_specs=pl.BlockSpec((tq, qcols), lambda i: (i, 0)),
        out_shape=jax.ShapeDtypeStruct((s, qcols), BF16),
        compiler_params=_params("arbitrary"),
        name="swa_attention",
    )(sinks, qkv, qkv, qkv)


def _pool_kernel(x_ref, xp_ref, g_ref, w_ref, sc_ref, o_ref, ext_ref, wb_ref, *, tm):
    i = pl.program_id(0)

    @pl.when(i == 0)
    def _():
        wb_ref[...] = w_ref[...].astype(BF16)

    g = g_ref[...]
    h = _rms(x_ref[...], g, NORM_EPS)
    hp = _rms(xp_ref[...], g, NORM_EPS)
    ext_ref[0:POOL_HALO, :] = jnp.where(i > 0, hp, 0.0)
    ext_ref[POOL_HALO:, :] = h
    t = i * tm + lax.broadcasted_iota(jnp.int32, (tm, 1), 0)
    for gi, win in enumerate(POOL_WINDOWS):
        cs = slice(gi * POOL_GROUP, (gi + 1) * POOL_GROUP)
        acc = ext_ref[POOL_HALO:POOL_HALO + tm, cs]
        for k in range(1, win):
            acc = acc + ext_ref[POOL_HALO - k:POOL_HALO - k + tm, cs]
        cnt = jnp.minimum(t + 1, win).astype(F32)
        pooled = acc / cnt - ext_ref[POOL_HALO:POOL_HALO + tm, cs]
        y = jnp.dot(pooled.astype(BF16), wb_ref[gi], preferred_element_type=F32)
        o_ref[:, cs] = x_ref[:, cs] + y * sc_ref[:, cs]


def _pool_mixer(x, gains, layer, pool_w, pool_scale, j):
    t, d = x.shape
    tm = POOL_TM
    ng = len(POOL_WINDOWS)
    g3 = gains.reshape(gains.shape[0], 1, d)
    sc3 = pool_scale.reshape(pool_scale.shape[0], 1, d)
    kern = functools.partial(_pool_kernel, tm=tm)
    return pl.pallas_call(
        kern,
        grid=(t // tm,),
        in_specs=[pl.BlockSpec((tm, d), lambda i: (i, 0)),
                  pl.BlockSpec((POOL_HALO, d), lambda i: (jnp.maximum(i * (tm // POOL_HALO) - 1, 0), 0)),
                  pl.BlockSpec((None, 1, d), lambda i: (layer, 0, 0)),
                  pl.BlockSpec((None, ng, POOL_GROUP, POOL_GROUP), lambda i: (j, 0, 0, 0)),
                  pl.BlockSpec((None, 1, d), lambda i: (j, 0, 0))],
        out_specs=pl.BlockSpec((tm, d), lambda i: (i, 0)),
        out_shape=jax.ShapeDtypeStruct((t, d), F32),
        scratch_shapes=[pltpu.VMEM((tm + POOL_HALO, d), F32),
                        pltpu.VMEM((ng, POOL_GROUP, POOL_GROUP), BF16)],
        compiler_params=_params("arbitrary"),
        name="pool_mixer",
    )(x, x, g3, pool_w, sc3)


def _ffn_kernel(start_ref, expert_ref, nsub_ref, nused_ref,
                x_hbm, g_ref, wg_ref, wu_ref, wd_ref, o_hbm,
                xs_ref, acc_ref, xbuf_ref, wgb_ref, wub_ref, wdb_ref, in_sem, out_sem, *,
                residual):
    s = pl.program_id(0)
    f = pl.program_id(1)
    nf = pl.num_programs(1)
    nsub = nsub_ref[s]
    row0 = start_ref[s] * FFN_SUB

    def in_copy(c, slot):
        return pltpu.make_async_copy(x_hbm.at[pl.ds(row0 + c * FFN_SUB, FFN_SUB)],
                                     xbuf_ref.at[slot], in_sem.at[slot])

    def out_copy(c):
        r = pl.multiple_of(c * FFN_SUB, FFN_SUB)
        return pltpu.make_async_copy(acc_ref.at[pl.ds(r, FFN_SUB)],
                                     o_hbm.at[pl.ds(row0 + c * FFN_SUB, FFN_SUB)], out_sem.at[0])

    @pl.when((nsub > 0) & (f == 0))
    def _():
        in_copy(0, 0).start()

        def load(c, carry):
            slot = c % 2

            @pl.when(c + 1 < nsub)
            def _():
                in_copy(c + 1, 1 - slot).start()

            in_copy(c, slot).wait()
            xr = xbuf_ref[slot]
            r = pl.multiple_of(c * FFN_SUB, FFN_SUB)
            xs_ref[pl.ds(r, FFN_SUB), :] = _rms(xr, g_ref[...], NORM_EPS).astype(BF16)
            acc_ref[pl.ds(r, FFN_SUB), :] = xr if residual else jnp.zeros_like(xr)
            return carry

        lax.fori_loop(0, nsub, load, 0)

    @pl.when(nsub > 0)
    def _():
        wgb_ref[...] = wg_ref[...].astype(BF16)
        wub_ref[...] = wu_ref[...].astype(BF16)
        wdb_ref[...] = wd_ref[...].astype(BF16)

        def sub(c, carry):
            r = pl.multiple_of(c * FFN_SUB, FFN_SUB)
            xc = xs_ref[pl.ds(r, FFN_SUB), :]
            gate = jnp.dot(xc, wgb_ref[...], preferred_element_type=F32)
            up = jnp.dot(xc, wub_ref[...], preferred_element_type=F32)
            act = (gate * jax.nn.sigmoid(gate) * up).astype(BF16)
            acc_ref[pl.ds(r, FFN_SUB), :] += jnp.dot(act, wdb_ref[...], preferred_element_type=F32)
            return carry

        lax.fori_loop(0, nsub, sub, 0)

    @pl.when((nsub > 0) & (f == nf - 1))
    def _():
        def start(c, carry):
            out_copy(c).start()
            return carry

        def wait(c, carry):
            out_copy(c).wait()
            return carry

        lax.fori_loop(0, nsub, start, 0)
        lax.fori_loop(0, nsub, wait, 0)

    @pl.when((s == pl.num_programs(0) - 1) & (f == nf - 1))
    def _():
        xbuf_ref[0] = jnp.zeros(xbuf_ref.shape[1:], F32)

        def zero_copy(c):
            return pltpu.make_async_copy(xbuf_ref.at[0], o_hbm.at[pl.ds(c * FFN_SUB, FFN_SUB)],
                                         out_sem.at[0])

        def start(c, carry):
            zero_copy(c).start()
            return carry

        def wait(c, carry):
            zero_copy(c).wait()
            return carry

        lax.fori_loop(nused_ref[1], o_hbm.shape[0] // FFN_SUB, start, 0)
        lax.fori_loop(nused_ref[1], o_hbm.shape[0] // FFN_SUB, wait, 0)


def _ffn(x_rows, gains, layer, w_gate, w_up, w_down, sb_start, sb_expert, sb_nsub, n_used, out_rows,
         residual):
    d = x_rows.shape[1]
    dff = w_gate.shape[2]
    nsb = sb_start.shape[0]
    nf = dff // FFN_TF
    rows = FFN_SUB * FFN_NSUB
    g3 = gains.reshape(gains.shape[0], 1, d)

    def w_in(s, f, st, ex, ns, nu):
        return (ex[s], 0, jnp.where(s < nu[0], f, nf - 1))

    def w_out(s, f, st, ex, ns, nu):
        return (ex[s], jnp.where(s < nu[0], f, nf - 1), 0)

    grid_spec = pltpu.PrefetchScalarGridSpec(
        num_scalar_prefetch=4,
        grid=(nsb, nf),
        in_specs=[pl.BlockSpec(memory_space=pl.ANY),
                  pl.BlockSpec((None, 1, d), lambda s, f, *_: (layer, 0, 0)),
                  pl.BlockSpec((None, d, FFN_TF), w_in),
                  pl.BlockSpec((None, d, FFN_TF), w_in),
                  pl.BlockSpec((None, FFN_TF, d), w_out)],
        out_specs=pl.BlockSpec(memory_space=pl.ANY),
        scratch_shapes=[pltpu.VMEM((rows, d), BF16),
                        pltpu.VMEM((rows, d), F32),
                        pltpu.VMEM((2, FFN_SUB, d), F32),
                        pltpu.VMEM((d, FFN_TF), BF16),
                        pltpu.VMEM((d, FFN_TF), BF16),
                        pltpu.VMEM((FFN_TF, d), BF16),
                        pltpu.SemaphoreType.DMA((2,)),
                        pltpu.SemaphoreType.DMA((1,))],
    )
    kern = functools.partial(_ffn_kernel, residual=residual)
    return pl.pallas_call(
        kern,
        grid_spec=grid_spec,
        out_shape=jax.ShapeDtypeStruct((out_rows, d), F32),
        compiler_params=pltpu.CompilerParams(dimension_semantics=("arbitrary", "arbitrary"),
                                             vmem_limit_bytes=60 * 1024 * 1024),
        name="ffn_residual" if residual else "ffn_experts",
    )(sb_start, sb_expert, sb_nsub, n_used, x_rows, g3, w_gate, w_up, w_down)


def _dense_ffn(x, gains, layer, w_gate, w_up, w_down, m):
    t = x.shape[0]
    total_sub = t // FFN_SUB
    nsb = -(-total_sub // FFN_NSUB)
    start = np.arange(nsb, dtype=np.int32) * FFN_NSUB
    nsub = np.minimum(FFN_NSUB, total_sub - start).astype(np.int32)
    return _ffn(x, gains, layer, w_gate, w_up, w_down,
                jnp.asarray(start), jnp.full((nsb,), m, jnp.int32), jnp.asarray(nsub),
                jnp.asarray([nsb, total_sub], jnp.int32), t, residual=True)


def _router_kernel(x_ref, g_ref, wr_ref, idx_ref, gate_ref, cnt_ref, carry_ref, *, tm):
    @pl.when(pl.program_id(0) == 0)
    def _():
        carry_ref[...] = jnp.zeros(carry_ref.shape, F32)

    h = _rms(x_ref[...], g_ref[...], NORM_EPS)
    logit = [jnp.sum(h * wr_ref[e:e + 1, :], axis=-1, keepdims=True) for e in range(N_EXPERTS)]

    def top(vals):
        best = vals[0]
        for v in vals[1:]:
            best = jnp.maximum(best, v)
        idx = jnp.full(best.shape, N_EXPERTS, jnp.int32)
        for e in reversed(range(N_EXPERTS)):
            idx = jnp.where(vals[e] == best, e, idx)
        return best, idx

    m1, i1 = top(logit)
    m2, i2 = top([jnp.where(i1 == e, -jnp.inf, logit[e]) for e in range(N_EXPERTS)])
    d = jnp.exp(m2 - m1)
    g1 = 1.0 / (1.0 + d)
    g2 = d / (1.0 + d)

    lane = lax.broadcasted_iota(jnp.int32, (tm, LANES), 1)
    onehot = ((lane == i1) | (lane == i2)).astype(BF16)
    r = lax.broadcasted_iota(jnp.int32, (tm, tm), 0)
    c = lax.broadcasted_iota(jnp.int32, (tm, tm), 1)
    before = (c < r).astype(BF16)
    rank = jnp.dot(before, onehot, preferred_element_type=F32) + carry_ref[...]
    r1 = jnp.sum(jnp.where(lane == i1, rank, 0.0), axis=-1, keepdims=True).astype(jnp.int32)
    r2 = jnp.sum(jnp.where(lane == i2, rank, 0.0), axis=-1, keepdims=True).astype(jnp.int32)
    carry_ref[...] += jnp.sum(onehot.astype(F32), axis=0, keepdims=True)
    cnt_ref[...] = carry_ref[...]

    l8 = lax.broadcasted_iota(jnp.int32, (tm, 8), 1)
    idx_ref[...] = jnp.where(l8 == 0, i1, jnp.where(l8 == 1, i2, jnp.where(l8 == 2, r1, r2)))
    gate_ref[...] = jnp.where(l8 == 0, g1, g2)


def _router(x, gains, layer, router_t, m):
    t, d = x.shape
    tm = ROUTER_TM
    g3 = gains.reshape(gains.shape[0], 1, d)
    kern = functools.partial(_router_kernel, tm=tm)
    return pl.pallas_call(
        kern,
        grid=(t // tm,),
        in_specs=[pl.BlockSpec((tm, d), lambda i: (i, 0)),
                  pl.BlockSpec((None, 1, d), lambda i: (layer, 0, 0)),
                  pl.BlockSpec((None, N_EXPERTS, d), lambda i: (m, 0, 0))],
        out_specs=[pl.BlockSpec((tm, 8), lambda i: (i, 0)),
                   pl.BlockSpec((tm, 8), lambda i: (i, 0)),
                   pl.BlockSpec((1, LANES), lambda i: (0, 0))],
        out_shape=[jax.ShapeDtypeStruct((t, 8), jnp.int32),
                   jax.ShapeDtypeStruct((t, 8), F32),
                   jax.ShapeDtypeStruct((1, LANES), F32)],
        scratch_shapes=[pltpu.VMEM((1, LANES), F32)],
        compiler_params=_params("arbitrary"),
        name="router",
    )(x, g3, router_t)


def _scatter_kernel(d0_ref, d1_ref, x_hbm, z_hbm, o_hbm, sem, *, ct):
    del z_hbm
    base = pl.program_id(0) * ct

    def issue(t, carry):
        tok = base + t
        pltpu.make_async_copy(x_hbm.at[pl.ds(tok, 1)], o_hbm.at[pl.ds(d0_ref[tok], 1)], sem.at[0]).start()
        pltpu.make_async_copy(x_hbm.at[pl.ds(tok, 1)], o_hbm.at[pl.ds(d1_ref[tok], 1)], sem.at[0]).start()
        return carry

    def wait(t, carry):
        pltpu.make_async_copy(x_hbm.at[pl.ds(0, 1)], o_hbm.at[pl.ds(0, 1)], sem.at[0]).wait()
        return carry

    lax.fori_loop(0, ct, issue, 0)
    lax.fori_loop(0, 2 * ct, wait, 0)


def _scatter_rows(x, d0, d1, n_rows):
    t, d = x.shape
    ct = SCATTER_CT
    kern = functools.partial(_scatter_kernel, ct=ct)
    grid_spec = pltpu.PrefetchScalarGridSpec(
        num_scalar_prefetch=2,
        grid=(t // ct,),
        in_specs=[pl.BlockSpec(memory_space=pl.ANY), pl.BlockSpec(memory_space=pl.ANY)],
        out_specs=pl.BlockSpec(memory_space=pl.ANY),
        scratch_shapes=[pltpu.SemaphoreType.DMA((1,))],
    )
    return pl.pallas_call(
        kern,
        grid_spec=grid_spec,
        out_shape=jax.ShapeDtypeStruct((n_rows, d), F32),
        input_output_aliases={3: 0},
        compiler_params=_params("arbitrary"),
        name="scatter_rows",
    )(d0, d1, x, jnp.zeros((n_rows, d), F32))


def _combine_kernel(d0_ref, d1_ref, x_ref, gate_ref, y_hbm, o_ref, buf_ref, sem, *, ct):
    base = pl.program_id(0) * ct

    def issue(t, carry):
        tok = base + t
        pltpu.make_async_copy(y_hbm.at[pl.ds(d0_ref[tok], 1)], buf_ref.at[0, pl.ds(t, 1)], sem.at[0]).start()
        pltpu.make_async_copy(y_hbm.at[pl.ds(d1_ref[tok], 1)], buf_ref.at[1, pl.ds(t, 1)], sem.at[0]).start()
        return carry

    def wait(t, carry):
        pltpu.make_async_copy(y_hbm.at[pl.ds(0, 1)], buf_ref.at[0, pl.ds(0, 1)], sem.at[0]).wait()
        return carry

    lax.fori_loop(0, ct, issue, 0)
    lax.fori_loop(0, 2 * ct, wait, 0)
    gts = gate_ref[...]
    o_ref[...] = x_ref[...] + (buf_ref[0] * gts[:, 0:1] + buf_ref[1] * gts[:, 1:2])


def _combine(x, y_rows, gates, d0, d1):
    t, d = x.shape
    ct = COMBINE_CT
    kern = functools.partial(_combine_kernel, ct=ct)
    grid_spec = pltpu.PrefetchScalarGridSpec(
        num_scalar_prefetch=2,
        grid=(t // ct,),
        in_specs=[pl.BlockSpec((ct, d), lambda i, *_: (i, 0)),
                  pl.BlockSpec((ct, 8), lambda i, *_: (i, 0)),
                  pl.BlockSpec(memory_space=pl.ANY)],
        out_specs=pl.BlockSpec((ct, d), lambda i, *_: (i, 0)),
        scratch_shapes=[pltpu.VMEM((2, ct, d), F32), pltpu.SemaphoreType.DMA((1,))],
    )
    return pl.pallas_call(
        kern,
        grid_spec=grid_spec,
        out_shape=jax.ShapeDtypeStruct((t, d), F32),
        compiler_params=_params("arbitrary"),
        name="moe_combine",
    )(d0, d1, x, gates, y_rows)


def _moe_ffn(x, gains, layer, router_w, w_gate, w_up, w_down, m):
    t, d = x.shape
    router_t = jnp.swapaxes(router_w, 1, 2)
    idx, gates, counts = _router(x, gains, layer, router_t, m)

    cnt = counts[0, :N_EXPERTS].astype(jnp.int32)
    nsub_e = (cnt + FFN_SUB - 1) // FFN_SUB
    start_e = jnp.cumsum(nsub_e) - nsub_e
    row_start = start_e * FFN_SUB
    d0 = row_start[idx[:, 0]] + idx[:, 2]
    d1 = row_start[idx[:, 1]] + idx[:, 3]

    max_sub = t * 2 // FFN_SUB + N_EXPERTS
    n_rows = max_sub * FFN_SUB
    nsb = -(-(t * 2 // FFN_SUB) // FFN_NSUB) + N_EXPERTS
    sbs_e = (nsub_e + FFN_NSUB - 1) // FFN_NSUB
    cum = jnp.cumsum(sbs_e)
    n_used = cum[-1]
    slot = jnp.arange(nsb, dtype=jnp.int32)
    e_of = jnp.minimum(jnp.sum((slot[:, None] >= cum[None, :]).astype(jnp.int32), axis=1), N_EXPERTS - 1)
    k_of = slot - (cum - sbs_e)[e_of]
    used = slot < n_used
    last_e = e_of[jnp.maximum(n_used - 1, 0)]
    sb_start = jnp.where(used, start_e[e_of] + k_of * FFN_NSUB, 0).astype(jnp.int32)
    sb_nsub = jnp.where(used, jnp.clip(nsub_e[e_of] - k_of * FFN_NSUB, 0, FFN_NSUB), 0).astype(jnp.int32)
    sb_expert = (m * N_EXPERTS + jnp.where(used, e_of, last_e)).astype(jnp.int32)

    flat = lambda w: w.reshape((-1,) + w.shape[2:])
    x_rows = _scatter_rows(x, d0, d1, n_rows)
    y_rows = _ffn(x_rows, gains, layer, flat(w_gate), flat(w_up), flat(w_down),
                  sb_start, sb_expert, sb_nsub,
                  jnp.stack([n_used, jnp.sum(nsub_e)]).astype(jnp.int32), n_rows, residual=False)
    return _combine(x, y_rows, gates, d0, d1)


def kernel(x, positions, norm1, norm2, final_norm, diff_w_qkv, diff_lambda_q1, diff_lambda_k1, diff_lambda_q2, diff_lambda_k2, diff_subln, diff_w_o, pool_w, pool_scale, swa_w_qkv, swa_b_qkv, swa_sinks, swa_w_o, swa_b_o, ffn_w_gate, ffn_w_up, ffn_w_down, moe_router, moe_w_gate, moe_w_up, moe_w_down):
    b, s, d = x.shape
    assert b == 1 and d == D_MODEL
    xs = x.reshape(s, d)
    pos_col = positions.reshape(s, 1)
    cos_a, sin_a = _rope_tables(pos_col, DIFF_DH)
    cos_c, sin_c = _rope_tables(pos_col, SWA_DH)
    zero_bias_qkv = jnp.zeros((1, diff_w_qkv.shape[2]), F32)
    zero_bias_o = jnp.zeros((1, d), F32)

    for i in range(N_LAYERS):
        kind, j = i % 3, i // 3
        if kind == 0:
            lambda_init = 0.8 - 0.6 * math.exp(-0.3 * i)
            h = _norm(xs, norm1, i, BF16)
            qkv = _mm_qkv(h, diff_w_qkv, j, zero_bias_qkv, cos_a, sin_a, half=DIFF_DH // 2,
                          scale=DIFF_DH ** -0.5, q_cols=2 * DIFF_HEADS * DIFF_DH,
                          k_cols=2 * DIFF_HEADS * DIFF_DH)
            att = _diff_attention(qkv, diff_lambda_q1, diff_lambda_k1, diff_lambda_q2, diff_lambda_k2,
                                  diff_subln, j, lambda_init)
            xs = _mm_residual(att, diff_w_o, j, zero_bias_o, xs)
        elif kind == 1:
            xs = _pool_mixer(xs, norm1, i, pool_w, pool_scale, j)
        else:
            h = _norm(xs, norm1, i, BF16)
            qkv = _mm_qkv(h, swa_w_qkv, j, swa_b_qkv[j][None, :], cos_c, sin_c, half=SWA_DH // 2,
                          scale=SWA_DH ** -0.5, q_cols=SWA_Q_HEADS * SWA_DH,
                          k_cols=SWA_KV_HEADS * SWA_DH)
            att = _swa_attention(qkv, swa_sinks[j])
            xs = _mm_residual(att, swa_w_o, j, swa_b_o[j][None, :], xs)
        m = i // 2
        if i % 2 == 0:
            xs = _dense_ffn(xs, norm2, i, ffn_w_gate, ffn_w_up, ffn_w_down, m)
        else:
            xs = _moe_ffn(xs, norm2, i, moe_router, moe_w_gate, moe_w_up, moe_w_down, m)
    out = _norm(xs, final_norm[None, :], 0, F32)
    return out.reshape(b, s, d)
```

```python
import functools
import math

import numpy as np
import jax
import jax.numpy as jnp
from jax import lax
from jax.experimental import pallas as pl
from jax.experimental.pallas import tpu as pltpu

F32 = jnp.float32
BF16 = jnp.bfloat16

D_MODEL = 2048
N_LAYERS = 4
NORM_EPS = 1e-6
SUBLN_EPS = 1e-5
MASK_VALUE = -1e30
ROPE_THETA = 10000.0

DIFF_HEADS = 8
DIFF_DH = 128
SWA_DH = 64
SWA_Q_HEADS = 32
SWA_KV_HEADS = 4
SWA_WINDOW = 128
POOL_WINDOWS = (2, 4, 8, 16)
POOL_GROUP = 512
POOL_HALO = 16
N_EXPERTS = 8

LANES = 128
VMEM_LIMIT = 56 * 1024 * 1024

MM_TM = 1024
MM_TN = 512
NORM_TM = 512
ATT_TQ = 1024
ATT_TK = 512
ATT_QC = 256
SWA_TQ = 256
POOL_TM = 512
FFN_SUB = 256
FFN_NSUB = 9
FFN_TF = 256
ROUTER_TM = 256
SCATTER_CT = 512
COMBINE_CT = 256


def _params(*sem):
    return pltpu.CompilerParams(dimension_semantics=sem, vmem_limit_bytes=VMEM_LIMIT)


def _rms(x, g, eps):
    return x * lax.rsqrt(jnp.mean(x * x, axis=-1, keepdims=True) + eps) * g


def _rope_table_kernel(pos_ref, inv_ref, sgn_ref, cos_ref, sin_ref):
    ang = pos_ref[...].astype(F32) * inv_ref[...]
    cos_ref[...] = jnp.cos(ang)
    sin_ref[...] = jnp.sin(ang) * sgn_ref[...]


def _rope_tables(pos_col, dh):
    s = pos_col.shape[0]
    half = dh // 2
    lane = np.arange(LANES)
    inv = (ROPE_THETA ** (-(lane % half).astype(np.float32) * 2.0 / dh)).astype(np.float32)
    sgn = np.where((lane % dh) < half, -1.0, 1.0).astype(np.float32)
    tm = 1024
    return pl.pallas_call(
        _rope_table_kernel,
        grid=(s // tm,),
        in_specs=[pl.BlockSpec((tm, 1), lambda i: (i, 0)),
                  pl.BlockSpec((1, LANES), lambda i: (0, 0)),
                  pl.BlockSpec((1, LANES), lambda i: (0, 0))],
        out_specs=[pl.BlockSpec((tm, LANES), lambda i: (i, 0))] * 2,
        out_shape=[jax.ShapeDtypeStruct((s, LANES), F32)] * 2,
        compiler_params=_params("arbitrary"),
        name="rope_tables",
    )(pos_col, jnp.asarray(inv)[None, :], jnp.asarray(sgn)[None, :])


def _rope_apply(x, cos, sin_signed, half):
    tn = x.shape[1]
    reps = tn // LANES
    c = jnp.concatenate([cos] * reps, axis=1)
    s = jnp.concatenate([sin_signed] * reps, axis=1)
    lane = lax.broadcasted_iota(jnp.int32, x.shape, 1)
    first = (lane % (2 * half)) < half
    swapped = jnp.where(first, pltpu.roll(x, tn - half, 1), pltpu.roll(x, half, 1))
    return x * c + swapped * s


def _norm_kernel(x_ref, g_ref, o_ref):
    o_ref[...] = _rms(x_ref[...], g_ref[...], NORM_EPS).astype(o_ref.dtype)


def _norm(x, gains, layer, out_dtype):
    t, d = x.shape
    g3 = gains.reshape(gains.shape[0], 1, d)
    return pl.pallas_call(
        _norm_kernel,
        grid=(t // NORM_TM,),
        in_specs=[pl.BlockSpec((NORM_TM, d), lambda i: (i, 0)),
                  pl.BlockSpec((None, 1, d), lambda i: (layer, 0, 0))],
        out_specs=pl.BlockSpec((NORM_TM, d), lambda i: (i, 0)),
        out_shape=jax.ShapeDtypeStruct((t, d), out_dtype),
        compiler_params=_params("arbitrary"),
        name="rmsnorm",
    )(x, g3)


def _mm_qkv_kernel(a_ref, w_ref, b_ref, cos_ref, sin_ref, o_ref, wb_ref, *,
                   half, scale, q_tiles, k_cols):
    jn = pl.program_id(0)

    @pl.when(pl.program_id(1) == 0)
    def _():
        wb_ref[...] = w_ref[...].astype(BF16)

    acc = jnp.dot(a_ref[...], wb_ref[...], preferred_element_type=F32) + b_ref[...]
    roped = _rope_apply(acc, cos_ref[...], sin_ref[...], half)

    @pl.when(jn < q_tiles)
    def _():
        o_ref[...] = (roped * scale).astype(o_ref.dtype)

    @pl.when(jn >= q_tiles)
    def _():
        rel = (jn - q_tiles) * acc.shape[1] + lax.broadcasted_iota(jnp.int32, acc.shape, 1)
        o_ref[...] = jnp.where(rel < k_cols, roped, acc).astype(o_ref.dtype)


def _mm_qkv(a, w, layer, bias, cos, sin, *, half, scale, q_cols, k_cols):
    m, k = a.shape
    n = w.shape[2]
    tm, tn = MM_TM, MM_TN
    assert q_cols % tn == 0
    kern = functools.partial(_mm_qkv_kernel, half=half, scale=scale, q_tiles=q_cols // tn,
                             k_cols=k_cols)
    return pl.pallas_call(
        kern,
        grid=(n // tn, m // tm),
        in_specs=[pl.BlockSpec((tm, k), lambda j, i: (i, 0)),
                  pl.BlockSpec((None, k, tn), lambda j, i: (layer, 0, j)),
                  pl.BlockSpec((1, tn), lambda j, i: (0, j)),
                  pl.BlockSpec((tm, LANES), lambda j, i: (i, 0)),
                  pl.BlockSpec((tm, LANES), lambda j, i: (i, 0))],
        out_specs=pl.BlockSpec((tm, tn), lambda j, i: (i, j)),
        out_shape=jax.ShapeDtypeStruct((m, n), BF16),
        scratch_shapes=[pltpu.VMEM((k, tn), BF16)],
        compiler_params=_params("arbitrary", "arbitrary"),
        name="qkv_proj",
    )(a, w, bias, cos, sin)


def _mm_res_kernel(a_ref, w_ref, b_ref, x_ref, o_ref, wb_ref):
    @pl.when(pl.program_id(1) == 0)
    def _():
        wb_ref[...] = w_ref[...].astype(BF16)

    acc = jnp.dot(a_ref[...], wb_ref[...], preferred_element_type=F32)
    o_ref[...] = x_ref[...] + (acc + b_ref[...])


def _mm_residual(a, w, layer, bias, x):
    m, k = a.shape
    n = w.shape[2]
    tm, tn = MM_TM, MM_TN
    return pl.pallas_call(
        _mm_res_kernel,
        grid=(n // tn, m // tm),
        in_specs=[pl.BlockSpec((tm, k), lambda j, i: (i, 0)),
                  pl.BlockSpec((None, k, tn), lambda j, i: (layer, 0, j)),
                  pl.BlockSpec((1, tn), lambda j, i: (0, j)),
                  pl.BlockSpec((tm, tn), lambda j, i: (i, j))],
        out_specs=pl.BlockSpec((tm, tn), lambda j, i: (i, j)),
        out_shape=jax.ShapeDtypeStruct((m, n), F32),
        scratch_shapes=[pltpu.VMEM((k, tn), BF16)],
        compiler_params=_params("arbitrary", "arbitrary"),
        name="out_proj",
    )(a, w, bias, x)


def _diff_attn_kernel(lq1_ref, lk1_ref, lq2_ref, lk2_ref, sub_ref,
                      q1_ref, q2_ref, k1_ref, k2_ref, v_ref, o_ref,
                      vt_ref, m1_ref, l1_ref, a1_ref, m2_ref, l2_ref, a2_ref, *, tq, tk, lambda_init):
    i = pl.program_id(1)
    qc = ATT_QC

    @pl.when(i == 0)
    def _():
        def transpose_tile(c, carry):
            rows = pl.ds(pl.multiple_of(c * tk, tk), tk)
            vt_ref[c] = v_ref[rows, :].astype(F32).T.astype(BF16)
            return carry

        lax.fori_loop(0, vt_ref.shape[0], transpose_tile, 0)

    for m_ref, l_ref, a_ref in ((m1_ref, l1_ref, a1_ref), (m2_ref, l2_ref, a2_ref)):
        m_ref[...] = jnp.full(m_ref.shape, MASK_VALUE, F32)
        l_ref[...] = jnp.zeros(l_ref.shape, F32)
        a_ref[...] = jnp.zeros(a_ref.shape, F32)
    q1t = q1_ref[...].astype(F32).T.astype(BF16)
    q2t = q2_ref[...].astype(F32).T.astype(BF16)

    def step(j, diag):
        ks = pl.ds(pl.multiple_of(j * tk, tk), tk)
        vt = vt_ref[j]
        for qt, k_ref, m_ref, l_ref, a_ref in ((q1t, k1_ref, m1_ref, l1_ref, a1_ref),
                                               (q2t, k2_ref, m2_ref, l2_ref, a2_ref)):
            kblk = k_ref[ks, :]
            for c in range(tq // qc):
                if diag is not None and (c + 1) * qc - 1 < diag * tk:
                    continue
                cs = slice(c * qc, (c + 1) * qc)
                s = jnp.dot(kblk, qt[:, cs], preferred_element_type=F32)
                if diag is not None and c * qc < (diag + 1) * tk - 1:
                    key = lax.broadcasted_iota(jnp.int32, s.shape, 0) + diag * tk
                    qry = lax.broadcasted_iota(jnp.int32, s.shape, 1) + c * qc
                    s = jnp.where(key <= qry, s, MASK_VALUE)
                m_old = m_ref[:, cs]
                m_new = jnp.maximum(m_old, jnp.max(s, axis=0, keepdims=True))
                p = jnp.exp2(s - m_new)
                alpha = jnp.exp2(m_old - m_new)
                l_ref[:, cs] = alpha * l_ref[:, cs] + jnp.sum(p, axis=0, keepdims=True)
                a_ref[:, cs] = alpha * a_ref[:, cs] + jnp.dot(vt, p.astype(BF16),
                                                              preferred_element_type=F32)
                m_ref[:, cs] = m_new

    def full_step(j, carry):
        step(j, None)
        return carry

    per_tile = tq // tk
    lax.fori_loop(0, i * per_tile, full_step, 0)
    for d in range(per_tile):
        step(i * per_tile + d, d)

    lam = (jnp.exp(jnp.sum(lq1_ref[...] * lk1_ref[...], axis=-1, keepdims=True))
           - jnp.exp(jnp.sum(lq2_ref[...] * lk2_ref[...], axis=-1, keepdims=True))
           + lambda_init)
    o = a1_ref[...] / l1_ref[...] - lam * (a2_ref[...] / l2_ref[...])
    o = o * lax.rsqrt(jnp.mean(o * o, axis=0, keepdims=True) + SUBLN_EPS) * sub_ref[...]
    o_ref[...] = (o * (1.0 - lambda_init)).T.astype(o_ref.dtype)


def _diff_attention(qkv, lq1, lk1, lq2, lk2, subln, layer, lambda_init):
    s = qkv.shape[0]
    tq = ATT_TQ
    dh, dv = DIFF_DH, 2 * DIFF_DH
    k_off = (2 * DIFF_HEADS * dh) // dh
    v_off = (4 * DIFF_HEADS * dh) // dv
    vec = lambda a: a.reshape(a.shape[0], 1, a.shape[1])
    lam_spec = pl.BlockSpec((None, 1, dh), lambda h, i: (layer, 0, 0))
    tk = ATT_TK
    kern = functools.partial(_diff_attn_kernel, tq=tq, tk=tk, lambda_init=lambda_init)
    return pl.pallas_call(
        kern,
        grid=(DIFF_HEADS, s // tq),
        in_specs=[lam_spec, lam_spec, lam_spec, lam_spec,
                  pl.BlockSpec((None, dv, 1), lambda h, i: (layer, 0, 0)),
                  pl.BlockSpec((tq, dh), lambda h, i: (i, 2 * h)),
                  pl.BlockSpec((tq, dh), lambda h, i: (i, 2 * h + 1)),
                  pl.BlockSpec((s, dh), lambda h, i: (0, k_off + 2 * h)),
                  pl.BlockSpec((s, dh), lambda h, i: (0, k_off + 2 * h + 1)),
                  pl.BlockSpec((s, dv), lambda h, i: (0, v_off + h))],
        out_specs=pl.BlockSpec((tq, dv), lambda h, i: (i, h)),
        out_shape=jax.ShapeDtypeStruct((s, DIFF_HEADS * dv), BF16),
        scratch_shapes=[pltpu.VMEM((s // tk, dv, tk), BF16),
                        pltpu.VMEM((1, tq), F32), pltpu.VMEM((1, tq), F32), pltpu.VMEM((dv, tq), F32),
                        pltpu.VMEM((1, tq), F32), pltpu.VMEM((1, tq), F32), pltpu.VMEM((dv, tq), F32)],
        compiler_params=_params("arbitrary", "arbitrary"),
        name="diff_attention",
    )(vec(lq1), vec(lk1), vec(lq2), vec(lk2), subln[:, :, None], qkv, qkv, qkv, qkv, qkv)


def _swa_attn_kernel(sinks_ref, q_ref, kv_ref, kvp_ref, o_ref, *, tq):
    w = SWA_WINDOW
    dh = SWA_DH
    pairs = SWA_Q_HEADS // SWA_KV_HEADS // 2
    tile = pl.program_id(0)
    kcols = SWA_KV_HEADS * dh

    row = lax.broadcasted_iota(jnp.int32, (pairs * w, 2 * w), 0) % w
    col = lax.broadcasted_iota(jnp.int32, (pairs * w, 2 * w), 1)
    band = (col > row) & (col <= row + w)
    lane = lax.broadcasted_iota(jnp.int32, (2 * w, LANES), 1)
    zeros = jnp.zeros((2 * w, LANES), BF16)

    for b in range(tq // w):
        if b == 0:
            prev, first_col = kvp_ref[...], jnp.where(tile > 0, 0, w)
        else:
            prev, first_col = kv_ref[(b - 1) * w:b * w, :], 0
        kvb = jnp.concatenate([prev, kv_ref[b * w:(b + 1) * w, :]], axis=0)
        valid = band & (col >= first_col)
        for hk in range(SWA_KV_HEADS):
            c0 = (hk // 2) * LANES
            kc = kvb[:, c0:c0 + LANES]
            vc = kvb[:, kcols + c0:kcols + c0 + LANES]
            own = (lane // dh) == (hk % 2)
            k_own = jnp.where(own, kc, zeros)
            v_own = jnp.where(own, vc, zeros)
            k_sw = pltpu.roll(k_own, dh, 1)
            v_sw = pltpu.roll(v_own, dh, 1)
            if hk % 2 == 0:
                k_even, k_odd, v_even, v_odd = k_own, k_sw, v_own, v_sw
            else:
                k_even, k_odd, v_even, v_odd = k_sw, k_own, v_sw, v_own
            kbd = jnp.concatenate([k_even, k_odd], axis=0)
            vbd = jnp.concatenate([v_even, v_odd], axis=0)
            qp = jnp.concatenate(
                [q_ref[b * w:(b + 1) * w, (hk * pairs + p) * LANES:(hk * pairs + p + 1) * LANES]
                 for p in range(pairs)], axis=0)
            s = lax.dot_general(qp, kbd, (((1,), (1,)), ((), ())), preferred_element_type=F32)
            ps = []
            for par in range(2):
                sh = jnp.where(valid, s[:, par * 2 * w:(par + 1) * 2 * w], MASK_VALUE)
                sink = jnp.concatenate(
                    [jnp.full((w, 1), sinks_ref[hk * 2 * pairs + 2 * p + par], F32)
                     for p in range(pairs)], axis=0)
                m = jnp.maximum(jnp.max(sh, axis=-1, keepdims=True), sink)
                e = jnp.exp(sh - m)
                denom = jnp.sum(e, axis=-1, keepdims=True) + jnp.exp(sink - m)
                ps.append((e / denom).astype(BF16))
            o = jnp.dot(jnp.concatenate(ps, axis=1), vbd, preferred_element_type=F32)
            for p in range(pairs):
                o_ref[b * w:(b + 1) * w, (hk * pairs + p) * LANES:(hk * pairs + p + 1) * LANES] = \
                    o[p * w:(p + 1) * w, :].astype(o_ref.dtype)


def _swa_attention(qkv, sinks):
    s = qkv.shape[0]
    tq = SWA_TQ
    qcols = SWA_Q_HEADS * SWA_DH
    kvcols = 2 * SWA_KV_HEADS * SWA_DH
    kern = functools.partial(_swa_attn_kernel, tq=tq)
    return pl.pallas_call(
        kern,
        grid=(s // tq,),
        in_specs=[pl.BlockSpec(memory_space=pltpu.SMEM),
                  pl.BlockSpec((tq, qcols), lambda i: (i, 0)),
                  pl.BlockSpec((tq, kvcols), lambda i: (i, qcols // kvcols)),
                  pl.BlockSpec((SWA_WINDOW, kvcols),
                               lambda i: (jnp.maximum(i * (tq // SWA_WINDOW) - 1, 0), qcols // kvcols))],
        out_specs=pl.BlockSpec((tq, qcols), lambda i: (i, 0)),
        out_shape=jax.ShapeDtypeStruct((s, qcols), BF16),
        compiler_params=_params("arbitrary"),
        name="swa_attention",
    )(sinks, qkv, qkv, qkv)


def _pool_kernel(x_ref, xp_ref, g_ref, w_ref, sc_ref, o_ref, ext_ref, wb_ref, *, tm):
    i = pl.program_id(0)

    @pl.when(i == 0)
    def _():
        wb_ref[...] = w_ref[...].astype(BF16)

    g = g_ref[...]
    h = _rms(x_ref[...], g, NORM_EPS)
    hp = _rms(xp_ref[...], g, NORM_EPS)
    ext_ref[0:POOL_HALO, :] = jnp.where(i > 0, hp, 0.0)
    ext_ref[POOL_HALO:, :] = h
    t = i * tm + lax.broadcasted_iota(jnp.int32, (tm, 1), 0)
    for gi, win in enumerate(POOL_WINDOWS):
        cs = slice(gi * POOL_GROUP, (gi + 1) * POOL_GROUP)
        acc = ext_ref[POOL_HALO:POOL_HALO + tm, cs]
        for k in range(1, win):
            acc = acc + ext_ref[POOL_HALO - k:POOL_HALO - k + tm, cs]
        cnt = jnp.minimum(t + 1, win).astype(F32)
        pooled = acc / cnt - ext_ref[POOL_HALO:POOL_HALO + tm, cs]
        y = jnp.dot(pooled.astype(BF16), wb_ref[gi], preferred_element_type=F32)
        o_ref[:, cs] = x_ref[:, cs] + y * sc_ref[:, cs]


def _pool_mixer(x, gains, layer, pool_w, pool_scale, j):
    t, d = x.shape
    tm = POOL_TM
    ng = len(POOL_WINDOWS)
    g3 = gains.reshape(gains.shape[0], 1, d)
    sc3 = pool_scale.reshape(pool_scale.shape[0], 1, d)
    kern = functools.partial(_pool_kernel, tm=tm)
    return pl.pallas_call(
        kern,
        grid=(t // tm,),
        in_specs=[pl.BlockSpec((tm, d), lambda i: (i, 0)),
                  pl.BlockSpec((POOL_HALO, d), lambda i: (jnp.maximum(i * (tm // POOL_HALO) - 1, 0), 0)),
                  pl.BlockSpec((None, 1, d), lambda i: (layer, 0, 0)),
                  pl.BlockSpec((None, ng, POOL_GROUP, POOL_GROUP), lambda i: (j, 0, 0, 0)),
                  pl.BlockSpec((None, 1, d), lambda i: (j, 0, 0))],
        out_specs=pl.BlockSpec((tm, d), lambda i: (i, 0)),
        out_shape=jax.ShapeDtypeStruct((t, d), F32),
        scratch_shapes=[pltpu.VMEM((tm + POOL_HALO, d), F32),
                        pltpu.VMEM((ng, POOL_GROUP, POOL_GROUP), BF16)],
        compiler_params=_params("arbitrary"),
        name="pool_mixer",
    )(x, x, g3, pool_w, sc3)


def _ffn_kernel(start_ref, expert_ref, nsub_ref, nused_ref,
                x_hbm, g_ref, wg_ref, wu_ref, wd_ref, o_hbm,
                xs_ref, acc_ref, xbuf_ref, wgb_ref, wub_ref, wdb_ref, in_sem, out_sem, *,
                residual):
    s = pl.program_id(0)
    f = pl.program_id(1)
    nf = pl.num_programs(1)
    nsub = nsub_ref[s]
    row0 = start_ref[s] * FFN_SUB

    def in_copy(c, slot):
        return pltpu.make_async_copy(x_hbm.at[pl.ds(row0 + c * FFN_SUB, FFN_SUB)],
                                     xbuf_ref.at[slot], in_sem.at[slot])

    def out_copy(c):
        r = pl.multiple_of(c * FFN_SUB, FFN_SUB)
        return pltpu.make_async_copy(acc_ref.at[pl.ds(r, FFN_SUB)],
                                     o_hbm.at[pl.ds(row0 + c * FFN_SUB, FFN_SUB)], out_sem.at[0])

    @pl.when((nsub > 0) & (f == 0))
    def _():
        in_copy(0, 0).start()

        def load(c, carry):
            slot = c % 2

            @pl.when(c + 1 < nsub)
            def _():
                in_copy(c + 1, 1 - slot).start()

            in_copy(c, slot).wait()
            xr = xbuf_ref[slot]
            r = pl.multiple_of(c * FFN_SUB, FFN_SUB)
            xs_ref[pl.ds(r, FFN_SUB), :] = _rms(xr, g_ref[...], NORM_EPS).astype(BF16)
            acc_ref[pl.ds(r, FFN_SUB), :] = xr if residual else jnp.zeros_like(xr)
            return carry

        lax.fori_loop(0, nsub, load, 0)

    @pl.when(nsub > 0)
    def _():
        wgb_ref[...] = wg_ref[...].astype(BF16)
        wub_ref[...] = wu_ref[...].astype(BF16)
        wdb_ref[...] = wd_ref[...].astype(BF16)

        def rows_step(first_sub, n_sub):
            rows = pl.ds(pl.multiple_of(first_sub * FFN_SUB, FFN_SUB), n_sub * FFN_SUB)
            xc = xs_ref[rows, :]
            gate = jnp.dot(xc, wgb_ref[...], preferred_element_type=F32)
            up = jnp.dot(xc, wub_ref[...], preferred_element_type=F32)
            act = (gate * jax.nn.sigmoid(gate) * up).astype(BF16)
            acc_ref[rows, :] += jnp.dot(act, wdb_ref[...], preferred_element_type=F32)

        def pair(cp, carry):
            rows_step(2 * cp, 2)
            return carry

        lax.fori_loop(0, nsub // 2, pair, 0)

        @pl.when(nsub % 2 == 1)
        def _():
            rows_step(nsub - 1, 1)

    @pl.when((nsub > 0) & (f == nf - 1))
    def _():
        def start(c, carry):
            out_copy(c).start()
            return carry

        def wait(c, carry):
            out_copy(c).wait()
            return carry

        lax.fori_loop(0, nsub, start, 0)
        lax.fori_loop(0, nsub, wait, 0)

    @pl.when((s == pl.num_programs(0) - 1) & (f == nf - 1))
    def _():
        xbuf_ref[0] = jnp.zeros(xbuf_ref.shape[1:], F32)

        def zero_copy(c):
            return pltpu.make_async_copy(xbuf_ref.at[0], o_hbm.at[pl.ds(c * FFN_SUB, FFN_SUB)],
                                         out_sem.at[0])

        def start(c, carry):
            zero_copy(c).start()
            return carry

        def wait(c, carry):
            zero_copy(c).wait()
            return carry

        lax.fori_loop(nused_ref[1], o_hbm.shape[0] // FFN_SUB, start, 0)
        lax.fori_loop(nused_ref[1], o_hbm.shape[0] // FFN_SUB, wait, 0)


def _ffn(x_rows, gains, layer, w_gate, w_up, w_down, sb_start, sb_expert, sb_nsub, n_used, out_rows,
         residual):
    d = x_rows.shape[1]
    dff = w_gate.shape[2]
    nsb = sb_start.shape[0]
    nf = dff // FFN_TF
    rows = FFN_SUB * FFN_NSUB
    g3 = gains.reshape(gains.shape[0], 1, d)

    def w_in(s, f, st, ex, ns, nu):
        return (ex[s], 0, jnp.where(s < nu[0], f, nf - 1))

    def w_out(s, f, st, ex, ns, nu):
        return (ex[s], jnp.where(s < nu[0], f, nf - 1), 0)

    grid_spec = pltpu.PrefetchScalarGridSpec(
        num_scalar_prefetch=4,
        grid=(nsb, nf),
        in_specs=[pl.BlockSpec(memory_space=pl.ANY),
                  pl.BlockSpec((None, 1, d), lambda s, f, *_: (layer, 0, 0)),
                  pl.BlockSpec((None, d, FFN_TF), w_in),
                  pl.BlockSpec((None, d, FFN_TF), w_in),
                  pl.BlockSpec((None, FFN_TF, d), w_out)],
        out_specs=pl.BlockSpec(memory_space=pl.ANY),
        scratch_shapes=[pltpu.VMEM((rows, d), BF16),
                        pltpu.VMEM((rows, d), F32),
                        pltpu.VMEM((2, FFN_SUB, d), F32),
                        pltpu.VMEM((d, FFN_TF), BF16),
                        pltpu.VMEM((d, FFN_TF), BF16),
                        pltpu.VMEM((FFN_TF, d), BF16),
                        pltpu.SemaphoreType.DMA((2,)),
                        pltpu.SemaphoreType.DMA((1,))],
    )
    kern = functools.partial(_ffn_kernel, residual=residual)
    return pl.pallas_call(
        kern,
        grid_spec=grid_spec,
        out_shape=jax.ShapeDtypeStruct((out_rows, d), F32),
        compiler_params=pltpu.CompilerParams(dimension_semantics=("arbitrary", "arbitrary"),
                                             vmem_limit_bytes=60 * 1024 * 1024),
        name="ffn_residual" if residual else "ffn_experts",
    )(sb_start, sb_expert, sb_nsub, n_used, x_rows, g3, w_gate, w_up, w_down)


def _dense_ffn(x, gains, layer, w_gate, w_up, w_down, m):
    t = x.shape[0]
    total_sub = t // FFN_SUB
    nsb = -(-total_sub // FFN_NSUB)
    start = np.arange(nsb, dtype=np.int32) * FFN_NSUB
    nsub = np.minimum(FFN_NSUB, total_sub - start).astype(np.int32)
    return _ffn(x, gains, layer, w_gate, w_up, w_down,
                jnp.asarray(start), jnp.full((nsb,), m, jnp.int32), jnp.asarray(nsub),
                jnp.asarray([nsb, total_sub], jnp.int32), t, residual=True)


def _router_kernel(x_ref, g_ref, wr_ref, idx_ref, gate_ref, cnt_ref, carry_ref, *, tm):
    @pl.when(pl.program_id(0) == 0)
    def _():
        carry_ref[...] = jnp.zeros(carry_ref.shape, F32)

    h = _rms(x_ref[...], g_ref[...], NORM_EPS)
    logit = [jnp.sum(h * wr_ref[e:e + 1, :], axis=-1, keepdims=True) for e in range(N_EXPERTS)]

    def top(vals):
        best = vals[0]
        for v in vals[1:]:
            best = jnp.maximum(best, v)
        idx = jnp.full(best.shape, N_EXPERTS, jnp.int32)
        for e in reversed(range(N_EXPERTS)):
            idx = jnp.where(vals[e] == best, e, idx)
        return best, idx

    m1, i1 = top(logit)
    m2, i2 = top([jnp.where(i1 == e, -jnp.inf, logit[e]) for e in range(N_EXPERTS)])
    d = jnp.exp(m2 - m1)
    g1 = 1.0 / (1.0 + d)
    g2 = d / (1.0 + d)

    lane = lax.broadcasted_iota(jnp.int32, (tm, LANES), 1)
    onehot = ((lane == i1) | (lane == i2)).astype(BF16)
    r = lax.broadcasted_iota(jnp.int32, (tm, tm), 0)
    c = lax.broadcasted_iota(jnp.int32, (tm, tm), 1)
    before = (c < r).astype(BF16)
    rank = jnp.dot(before, onehot, preferred_element_type=F32) + carry_ref[...]
    r1 = jnp.sum(jnp.where(lane == i1, rank, 0.0), axis=-1, keepdims=True).astype(jnp.int32)
    r2 = jnp.sum(jnp.where(lane == i2, rank, 0.0), axis=-1, keepdims=True).astype(jnp.int32)
    carry_ref[...] += jnp.sum(onehot.astype(F32), axis=0, keepdims=True)
    cnt_ref[...] = carry_ref[...]

    l8 = lax.broadcasted_iota(jnp.int32, (tm, 8), 1)
    idx_ref[...] = jnp.where(l8 == 0, i1, jnp.where(l8 == 1, i2, jnp.where(l8 == 2, r1, r2)))
    gate_ref[...] = jnp.where(l8 == 0, g1, g2)


def _router(x, gains, layer, router_t, m):
    t, d = x.shape
    tm = ROUTER_TM
    g3 = gains.reshape(gains.shape[0], 1, d)
    kern = functools.partial(_router_kernel, tm=tm)
    return pl.pallas_call(
        kern,
        grid=(t // tm,),
        in_specs=[pl.BlockSpec((tm, d), lambda i: (i, 0)),
                  pl.BlockSpec((None, 1, d), lambda i: (layer, 0, 0)),
                  pl.BlockSpec((None, N_EXPERTS, d), lambda i: (m, 0, 0))],
        out_specs=[pl.BlockSpec((tm, 8), lambda i: (i, 0)),
                   pl.BlockSpec((tm, 8), lambda i: (i, 0)),
                   pl.BlockSpec((1, LANES), lambda i: (0, 0))],
        out_shape=[jax.ShapeDtypeStruct((t, 8), jnp.int32),
                   jax.ShapeDtypeStruct((t, 8), F32),
                   jax.ShapeDtypeStruct((1, LANES), F32)],
        scratch_shapes=[pltpu.VMEM((1, LANES), F32)],
        compiler_params=_params("arbitrary"),
        name="router",
    )(x, g3, router_t)


def _scatter_kernel(d0_ref, d1_ref, x_ref, z_hbm, o_hbm, sem, *, ct):
    del z_hbm
    base = pl.program_id(0) * ct

    def issue(t, carry):
        tok = base + t
        pltpu.make_async_copy(x_ref.at[pl.ds(t, 1)], o_hbm.at[pl.ds(d0_ref[tok], 1)], sem.at[0]).start()
        pltpu.make_async_copy(x_ref.at[pl.ds(t, 1)], o_hbm.at[pl.ds(d1_ref[tok], 1)], sem.at[0]).start()
        return carry

    def wait(t, carry):
        pltpu.make_async_copy(x_ref.at[pl.ds(0, 1)], o_hbm.at[pl.ds(0, 1)], sem.at[0]).wait()
        return carry

    lax.fori_loop(0, ct, issue, 0)
    lax.fori_loop(0, 2 * ct, wait, 0)


def _scatter_rows(x, d0, d1, n_rows):
    t, d = x.shape
    ct = SCATTER_CT
    kern = functools.partial(_scatter_kernel, ct=ct)
    grid_spec = pltpu.PrefetchScalarGridSpec(
        num_scalar_prefetch=2,
        grid=(t // ct,),
        in_specs=[pl.BlockSpec((ct, d), lambda i, *_: (i, 0)), pl.BlockSpec(memory_space=pl.ANY)],
        out_specs=pl.BlockSpec(memory_space=pl.ANY),
        scratch_shapes=[pltpu.SemaphoreType.DMA((1,))],
    )
    return pl.pallas_call(
        kern,
        grid_spec=grid_spec,
        out_shape=jax.ShapeDtypeStruct((n_rows, d), F32),
        input_output_aliases={3: 0},
        compiler_params=_params("arbitrary"),
        name="scatter_rows",
    )(d0, d1, x, jnp.zeros((n_rows, d), F32))


def _combine_kernel(d0_ref, d1_ref, x_ref, gate_ref, y_hbm, o_ref, buf_ref, sem, *, ct):
    base = pl.program_id(0) * ct

    def issue(t, carry):
        tok = base + t
        pltpu.make_async_copy(y_hbm.at[pl.ds(d0_ref[tok], 1)], buf_ref.at[0, pl.ds(t, 1)], sem.at[0]).start()
        pltpu.make_async_copy(y_hbm.at[pl.ds(d1_ref[tok], 1)], buf_ref.at[1, pl.ds(t, 1)], sem.at[0]).start()
        return carry

    def wait(t, carry):
        pltpu.make_async_copy(y_hbm.at[pl.ds(0, 1)], buf_ref.at[0, pl.ds(0, 1)], sem.at[0]).wait()
        return carry

    lax.fori_loop(0, ct, issue, 0)
    lax.fori_loop(0, 2 * ct, wait, 0)
    gts = gate_ref[...]
    o_ref[...] = x_ref[...] + (buf_ref[0] * gts[:, 0:1] + buf_ref[1] * gts[:, 1:2])


def _combine(x, y_rows, gates, d0, d1):
    t, d = x.shape
    ct = COMBINE_CT
    kern = functools.partial(_combine_kernel, ct=ct)
    grid_spec = pltpu.PrefetchScalarGridSpec(
        num_scalar_prefetch=2,
        grid=(t // ct,),
        in_specs=[pl.BlockSpec((ct, d), lambda i, *_: (i, 0)),
                  pl.BlockSpec((ct, 8), lambda i, *_: (i, 0)),
                  pl.BlockSpec(memory_space=pl.ANY)],
        out_specs=pl.BlockSpec((ct, d), lambda i, *_: (i, 0)),
        scratch_shapes=[pltpu.VMEM((2, ct, d), F32), pltpu.SemaphoreType.DMA((1,))],
    )
    return pl.pallas_call(
        kern,
        grid_spec=grid_spec,
        out_shape=jax.ShapeDtypeStruct((t, d), F32),
        compiler_params=_params("arbitrary"),
        name="moe_combine",
    )(d0, d1, x, gates, y_rows)


def _moe_ffn(x, gains, layer, router_w, w_gate, w_up, w_down, m):
    t, d = x.shape
    router_t = jnp.swapaxes(router_w, 1, 2)
    idx, gates, counts = _router(x, gains, layer, router_t, m)

    cnt = counts[0, :N_EXPERTS].astype(jnp.int32)
    nsub_e = (cnt + FFN_SUB - 1) // FFN_SUB
    start_e = jnp.cumsum(nsub_e) - nsub_e
    row_start = start_e * FFN_SUB
    d0 = row_start[idx[:, 0]] + idx[:, 2]
    d1 = row_start[idx[:, 1]] + idx[:, 3]

    max_sub = t * 2 // FFN_SUB + N_EXPERTS
    n_rows = max_sub * FFN_SUB
    nsb = -(-(t * 2 // FFN_SUB) // FFN_NSUB) + N_EXPERTS
    sbs_e = (nsub_e + FFN_NSUB - 1) // FFN_NSUB
    cum = jnp.cumsum(sbs_e)
    n_used = cum[-1]
    slot = jnp.arange(nsb, dtype=jnp.int32)
    e_of = jnp.minimum(jnp.sum((slot[:, None] >= cum[None, :]).astype(jnp.int32), axis=1), N_EXPERTS - 1)
    k_of = slot - (cum - sbs_e)[e_of]
    used = slot < n_used
    last_e = e_of[jnp.maximum(n_used - 1, 0)]
    sb_start = jnp.where(used, start_e[e_of] + k_of * FFN_NSUB, 0).astype(jnp.int32)
    sb_nsub = jnp.where(used, jnp.clip(nsub_e[e_of] - k_of * FFN_NSUB, 0, FFN_NSUB), 0).astype(jnp.int32)
    sb_expert = (m * N_EXPERTS + jnp.where(used, e_of, last_e)).astype(jnp.int32)

    flat = lambda w: w.reshape((-1,) + w.shape[2:])
    x_rows = _scatter_rows(x, d0, d1, n_rows)
    y_rows = _ffn(x_rows, gains, layer, flat(w_gate), flat(w_up), flat(w_down),
                  sb_start, sb_expert, sb_nsub,
                  jnp.stack([n_used, jnp.sum(nsub_e)]).astype(jnp.int32), n_rows, residual=False)
    return _combine(x, y_rows, gates, d0, d1)


def kernel(x, positions, norm1, norm2, final_norm, diff_w_qkv, diff_lambda_q1, diff_lambda_k1, diff_lambda_q2, diff_lambda_k2, diff_subln, diff_w_o, pool_w, pool_scale, swa_w_qkv, swa_b_qkv, swa_sinks, swa_w_o, swa_b_o, ffn_w_gate, ffn_w_up, ffn_w_down, moe_router, moe_w_gate, moe_w_up, moe_w_down):
    b, s, d = x.shape
    assert b == 1 and d == D_MODEL
    xs = x.reshape(s, d)
    pos_col = positions.reshape(s, 1)
    cos_a, sin_a = _rope_tables(pos_col, DIFF_DH)
    cos_c, sin_c = _rope_tables(pos_col, SWA_DH)
    zero_bias_qkv = jnp.zeros((1, diff_w_qkv.shape[2]), F32)
    zero_bias_o = jnp.zeros((1, d), F32)

    for i in range(N_LAYERS):
        kind, j = i % 3, i // 3
        if kind == 0:
            lambda_init = 0.8 - 0.6 * math.exp(-0.3 * i)
            h = _norm(xs, norm1, i, BF16)
            qkv = _mm_qkv(h, diff_w_qkv, j, zero_bias_qkv, cos_a, sin_a, half=DIFF_DH // 2,
                          scale=DIFF_DH ** -0.5 * math.log2(math.e), q_cols=2 * DIFF_HEADS * DIFF_DH,
                          k_cols=2 * DIFF_HEADS * DIFF_DH)
            att = _diff_attention(qkv, diff_lambda_q1, diff_lambda_k1, diff_lambda_q2, diff_lambda_k2,
                                  diff_subln, j, lambda_init)
            xs = _mm_residual(att, diff_w_o, j, zero_bias_o, xs)
        elif kind == 1:
            xs = _pool_mixer(xs, norm1, i, pool_w, pool_scale, j)
        else:
            h = _norm(xs, norm1, i, BF16)
            qkv = _mm_qkv(h, swa_w_qkv, j, swa_b_qkv[j][None, :], cos_c, sin_c, half=SWA_DH // 2,
                          scale=SWA_DH ** -0.5, q_cols=SWA_Q_HEADS * SWA_DH,
                          k_cols=SWA_KV_HEADS * SWA_DH)
            att = _swa_attention(qkv, swa_sinks[j])
            xs = _mm_residual(att, swa_w_o, j, swa_b_o[j][None, :], xs)
        m = i // 2
        if i % 2 == 0:
            xs = _dense_ffn(xs, norm2, i, ffn_w_gate, ffn_w_up, ffn_w_down, m)
        else:
            xs = _moe_ffn(xs, norm2, i, moe_router, moe_w_gate, moe_w_up, moe_w_down, m)
    out = _norm(xs, final_norm[None, :], 0, F32)
    return out.reshape(b, s, d)
```

```python
import functools
import math

import numpy as np
import jax
import jax.numpy as jnp
from jax import lax
from jax.experimental import pallas as pl
from jax.experimental.pallas import tpu as pltpu

F32 = jnp.float32
BF16 = jnp.bfloat16

D_MODEL = 2048
N_LAYERS = 4
NORM_EPS = 1e-6
SUBLN_EPS = 1e-5
MASK_VALUE = -1e30
ROPE_THETA = 10000.0

DIFF_HEADS = 8
DIFF_DH = 128
SWA_DH = 64
SWA_Q_HEADS = 32
SWA_KV_HEADS = 4
SWA_WINDOW = 128
POOL_WINDOWS = (2, 4, 8, 16)
POOL_GROUP = 512
POOL_HALO = 16
N_EXPERTS = 8

LANES = 128
VMEM_LIMIT = 56 * 1024 * 1024

MM_TM = 2048
MM_TN = 512
MM_CHUNK = 256
NORM_TM = 512
ATT_TQ = 2048
ATT_TK = 512
ATT_QC = 256
SWA_TQ = 256
POOL_TM = 512
FFN_SUB = 256
FFN_NSUB = 9
FFN_CHAIN = 4
FFN_TF = 256
ROUTER_TM = 256
SCATTER_CT = 512
COMBINE_CT = 256


def _params(*sem):
    return pltpu.CompilerParams(dimension_semantics=sem, vmem_limit_bytes=VMEM_LIMIT)


def _rms(x, g, eps):
    return x * lax.rsqrt(jnp.mean(x * x, axis=-1, keepdims=True) + eps) * g


def _rope_table_kernel(pos_ref, inv_ref, sgn_ref, cos_ref, sin_ref):
    ang = pos_ref[...].astype(F32) * inv_ref[...]
    cos_ref[...] = jnp.cos(ang)
    sin_ref[...] = jnp.sin(ang) * sgn_ref[...]


def _rope_tables(pos_col, dh):
    s = pos_col.shape[0]
    half = dh // 2
    lane = np.arange(LANES)
    inv = (ROPE_THETA ** (-(lane % half).astype(np.float32) * 2.0 / dh)).astype(np.float32)
    sgn = np.where((lane % dh) < half, -1.0, 1.0).astype(np.float32)
    tm = 1024
    return pl.pallas_call(
        _rope_table_kernel,
        grid=(s // tm,),
        in_specs=[pl.BlockSpec((tm, 1), lambda i: (i, 0)),
                  pl.BlockSpec((1, LANES), lambda i: (0, 0)),
                  pl.BlockSpec((1, LANES), lambda i: (0, 0))],
        out_specs=[pl.BlockSpec((tm, LANES), lambda i: (i, 0))] * 2,
        out_shape=[jax.ShapeDtypeStruct((s, LANES), F32)] * 2,
        compiler_params=_params("arbitrary"),
        name="rope_tables",
    )(pos_col, jnp.asarray(inv)[None, :], jnp.asarray(sgn)[None, :])


def _swap_halves(x, half):
    if 2 * half == LANES:
        return pltpu.roll(x, half, 1)
    lane = lax.broadcasted_iota(jnp.int32, x.shape, 1)
    first = (lane % (2 * half)) < half
    return jnp.where(first, pltpu.roll(x, LANES - half, 1), pltpu.roll(x, half, 1))


def _norm_kernel(x_ref, g_ref, o_ref):
    o_ref[...] = _rms(x_ref[...], g_ref[...], NORM_EPS).astype(o_ref.dtype)


def _norm(x, gains, layer, out_dtype):
    t, d = x.shape
    g3 = gains.reshape(gains.shape[0], 1, d)
    return pl.pallas_call(
        _norm_kernel,
        grid=(t // NORM_TM,),
        in_specs=[pl.BlockSpec((NORM_TM, d), lambda i: (i, 0)),
                  pl.BlockSpec((None, 1, d), lambda i: (layer, 0, 0))],
        out_specs=pl.BlockSpec((NORM_TM, d), lambda i: (i, 0)),
        out_shape=jax.ShapeDtypeStruct((t, d), out_dtype),
        compiler_params=_params("arbitrary"),
        name="rmsnorm",
    )(x, g3)


def _mm_qkv_kernel(a_ref, w_ref, b_ref, cos_ref, sin_ref, o_ref, wb_ref, *,
                   half, scale, q_cols, k_cols):
    tn = o_ref.shape[1]
    col0 = pl.program_id(0) * tn

    @pl.when(pl.program_id(1) == 0)
    def _():
        wb_ref[...] = w_ref[...].astype(BF16)

    cos = cos_ref[...]
    sin = sin_ref[...]
    for c in range(tn // MM_CHUNK):
        cs = slice(c * MM_CHUNK, (c + 1) * MM_CHUNK)
        col = col0 + c * MM_CHUNK
        rot = jnp.where(col < q_cols + k_cols, 1.0, 0.0)
        sc = jnp.where(col < q_cols, scale, 1.0)
        cf = (1.0 + rot * (cos - 1.0)) * sc
        sf = sin * (rot * sc)
        acc = jnp.dot(a_ref[...], wb_ref[:, cs], preferred_element_type=F32) + b_ref[:, cs]
        out = []
        for g in range(MM_CHUNK // LANES):
            xg = acc[:, g * LANES:(g + 1) * LANES]
            out.append(xg * cf + _swap_halves(xg, half) * sf)
        o_ref[:, cs] = jnp.concatenate(out, axis=1).astype(o_ref.dtype)


def _mm_qkv(a, w, layer, bias, cos, sin, *, half, scale, q_cols, k_cols):
    m, k = a.shape
    n = w.shape[2]
    tm, tn = MM_TM, MM_TN
    assert q_cols % MM_CHUNK == 0 and k_cols % MM_CHUNK == 0 and tn % MM_CHUNK == 0
    kern = functools.partial(_mm_qkv_kernel, half=half, scale=scale, q_cols=q_cols, k_cols=k_cols)
    return pl.pallas_call(
        kern,
        grid=(n // tn, m // tm),
        in_specs=[pl.BlockSpec((tm, k), lambda j, i: (i, 0)),
                  pl.BlockSpec((None, k, tn), lambda j, i: (layer, 0, j)),
                  pl.BlockSpec((1, tn), lambda j, i: (0, j)),
                  pl.BlockSpec((tm, LANES), lambda j, i: (i, 0)),
                  pl.BlockSpec((tm, LANES), lambda j, i: (i, 0))],
        out_specs=pl.BlockSpec((tm, tn), lambda j, i: (i, j)),
        out_shape=jax.ShapeDtypeStruct((m, n), BF16),
        scratch_shapes=[pltpu.VMEM((k, tn), BF16)],
        compiler_params=_params("arbitrary", "arbitrary"),
        name="qkv_proj",
    )(a, w, bias, cos, sin)


def _mm_res_kernel(a_ref, w_ref, b_ref, x_ref, o_ref, wb_ref):
    @pl.when(pl.program_id(1) == 0)
    def _():
        wb_ref[...] = w_ref[...].astype(BF16)

    acc = jnp.dot(a_ref[...], wb_ref[...], preferred_element_type=F32)
    o_ref[...] = x_ref[...] + (acc + b_ref[...])


def _mm_residual(a, w, layer, bias, x):
    m, k = a.shape
    n = w.shape[2]
    tm, tn = MM_TM, MM_TN
    return pl.pallas_call(
        _mm_res_kernel,
        grid=(n // tn, m // tm),
        in_specs=[pl.BlockSpec((tm, k), lambda j, i: (i, 0)),
                  pl.BlockSpec((None, k, tn), lambda j, i: (layer, 0, j)),
                  pl.BlockSpec((1, tn), lambda j, i: (0, j)),
                  pl.BlockSpec((tm, tn), lambda j, i: (i, j))],
        out_specs=pl.BlockSpec((tm, tn), lambda j, i: (i, j)),
        out_shape=jax.ShapeDtypeStruct((m, n), F32),
        scratch_shapes=[pltpu.VMEM((k, tn), BF16)],
        compiler_params=_params("arbitrary", "arbitrary"),
        name="out_proj",
    )(a, w, bias, x)


def _diff_attn_kernel(lq1_ref, lk1_ref, lq2_ref, lk2_ref, sub_ref,
                      q1_ref, q2_ref, k1_ref, k2_ref, v_ref, o_ref,
                      vt_ref, m1_ref, l1_ref, a1_ref, m2_ref, l2_ref, a2_ref, *, tq, tk, lambda_init):
    i = pl.program_id(1)
    qc = ATT_QC

    @pl.when(i == 0)
    def _():
        def transpose_tile(c, carry):
            rows = pl.ds(pl.multiple_of(c * tk, tk), tk)
            vt_ref[c] = v_ref[rows, :].astype(F32).T.astype(BF16)
            return carry

        lax.fori_loop(0, vt_ref.shape[0], transpose_tile, 0)

    for m_ref, l_ref, a_ref in ((m1_ref, l1_ref, a1_ref), (m2_ref, l2_ref, a2_ref)):
        m_ref[...] = jnp.full(m_ref.shape, MASK_VALUE, F32)
        l_ref[...] = jnp.zeros(l_ref.shape, F32)
        a_ref[...] = jnp.zeros(a_ref.shape, F32)
    q1t = q1_ref[...].astype(F32).T.astype(BF16)
    q2t = q2_ref[...].astype(F32).T.astype(BF16)

    def step(j, diag):
        ks = pl.ds(pl.multiple_of(j * tk, tk), tk)
        vt = vt_ref[j]
        for qt, k_ref, m_ref, l_ref, a_ref in ((q1t, k1_ref, m1_ref, l1_ref, a1_ref),
                                               (q2t, k2_ref, m2_ref, l2_ref, a2_ref)):
            kblk = k_ref[ks, :]
            for c in range(tq // qc):
                if diag is not None and (c + 1) * qc - 1 < diag * tk:
                    continue
                cs = slice(c * qc, (c + 1) * qc)
                s = jnp.dot(kblk, qt[:, cs], preferred_element_type=F32)
                if diag is not None and c * qc < (diag + 1) * tk - 1:
                    key = lax.broadcasted_iota(jnp.int32, s.shape, 0) + diag * tk
                    qry = lax.broadcasted_iota(jnp.int32, s.shape, 1) + c * qc
                    s = jnp.where(key <= qry, s, MASK_VALUE)
                m_old = m_ref[:, cs]
                m_new = jnp.maximum(m_old, jnp.max(s, axis=0, keepdims=True))
                p = jnp.exp2(s - m_new)
                alpha = jnp.exp2(m_old - m_new)
                l_ref[:, cs] = alpha * l_ref[:, cs] + jnp.sum(p, axis=0, keepdims=True)
                a_ref[:, cs] = alpha * a_ref[:, cs] + jnp.dot(vt, p.astype(BF16),
                                                              preferred_element_type=F32)
                m_ref[:, cs] = m_new

    def full_step(j, carry):
        step(j, None)
        return carry

    per_tile = tq // tk
    lax.fori_loop(0, i * per_tile, full_step, 0)
    for d in range(per_tile):
        step(i * per_tile + d, d)

    lam = (jnp.exp(jnp.sum(lq1_ref[...] * lk1_ref[...], axis=-1, keepdims=True))
           - jnp.exp(jnp.sum(lq2_ref[...] * lk2_ref[...], axis=-1, keepdims=True))
           + lambda_init)
    o = a1_ref[...] / l1_ref[...] - lam * (a2_ref[...] / l2_ref[...])
    o = o * lax.rsqrt(jnp.mean(o * o, axis=0, keepdims=True) + SUBLN_EPS) * sub_ref[...]
    o_ref[...] = (o * (1.0 - lambda_init)).T.astype(o_ref.dtype)


def _diff_attention(qkv, lq1, lk1, lq2, lk2, subln, layer, lambda_init):
    s = qkv.shape[0]
    tq = ATT_TQ
    dh, dv = DIFF_DH, 2 * DIFF_DH
    k_off = (2 * DIFF_HEADS * dh) // dh
    v_off = (4 * DIFF_HEADS * dh) // dv
    vec = lambda a: a.reshape(a.shape[0], 1, a.shape[1])
    lam_spec = pl.BlockSpec((None, 1, dh), lambda h, i: (layer, 0, 0))
    tk = ATT_TK
    kern = functools.partial(_diff_attn_kernel, tq=tq, tk=tk, lambda_init=lambda_init)
    return pl.pallas_call(
        kern,
        grid=(DIFF_HEADS, s // tq),
        in_specs=[lam_spec, lam_spec, lam_spec, lam_spec,
                  pl.BlockSpec((None, dv, 1), lambda h, i: (layer, 0, 0)),
                  pl.BlockSpec((tq, dh), lambda h, i: (i, 2 * h)),
                  pl.BlockSpec((tq, dh), lambda h, i: (i, 2 * h + 1)),
                  pl.BlockSpec((s, dh), lambda h, i: (0, k_off + 2 * h)),
                  pl.BlockSpec((s, dh), lambda h, i: (0, k_off + 2 * h + 1)),
                  pl.BlockSpec((s, dv), lambda h, i: (0, v_off + h))],
        out_specs=pl.BlockSpec((tq, dv), lambda h, i: (i, h)),
        out_shape=jax.ShapeDtypeStruct((s, DIFF_HEADS * dv), BF16),
        scratch_shapes=[pltpu.VMEM((s // tk, dv, tk), BF16),
                        pltpu.VMEM((1, tq), F32), pltpu.VMEM((1, tq), F32), pltpu.VMEM((dv, tq), F32),
                        pltpu.VMEM((1, tq), F32), pltpu.VMEM((1, tq), F32), pltpu.VMEM((dv, tq), F32)],
        compiler_params=_params("arbitrary", "arbitrary"),
        name="diff_attention",
    )(vec(lq1), vec(lk1), vec(lq2), vec(lk2), subln[:, :, None], qkv, qkv, qkv, qkv, qkv)


def _swa_attn_kernel(sinks_ref, q_ref, kv_ref, kvp_ref, o_ref, *, tq):
    w = SWA_WINDOW
    dh = SWA_DH
    pairs = SWA_Q_HEADS // SWA_KV_HEADS // 2
    tile = pl.program_id(0)
    kcols = SWA_KV_HEADS * dh

    row = lax.broadcasted_iota(jnp.int32, (pairs * w, 2 * w), 0) % w
    col = lax.broadcasted_iota(jnp.int32, (pairs * w, 2 * w), 1)
    band = (col > row) & (col <= row + w)
    lane = lax.broadcasted_iota(jnp.int32, (2 * w, LANES), 1)
    zeros = jnp.zeros((2 * w, LANES), BF16)

    for b in range(tq // w):
        if b == 0:
            prev, first_col = kvp_ref[...], jnp.where(tile > 0, 0, w)
        else:
            prev, first_col = kv_ref[(b - 1) * w:b * w, :], 0
        kvb = jnp.concatenate([prev, kv_ref[b * w:(b + 1) * w, :]], axis=0)
        valid = band & (col >= first_col)
        for hk in range(SWA_KV_HEADS):
            c0 = (hk // 2) * LANES
            kc = kvb[:, c0:c0 + LANES]
            vc = kvb[:, kcols + c0:kcols + c0 + LANES]
            own = (lane // dh) == (hk % 2)
            k_own = jnp.where(own, kc, zeros)
            v_own = jnp.where(own, vc, zeros)
            k_sw = pltpu.roll(k_own, dh, 1)
            v_sw = pltpu.roll(v_own, dh, 1)
            if hk % 2 == 0:
                k_even, k_odd, v_even, v_odd = k_own, k_sw, v_own, v_sw
            else:
                k_even, k_odd, v_even, v_odd = k_sw, k_own, v_sw, v_own
            kbd = jnp.concatenate([k_even, k_odd], axis=0)
            vbd = jnp.concatenate([v_even, v_odd], axis=0)
            qp = jnp.concatenate(
                [q_ref[b * w:(b + 1) * w, (hk * pairs + p) * LANES:(hk * pairs + p + 1) * LANES]
                 for p in range(pairs)], axis=0)
            s = lax.dot_general(qp, kbd, (((1,), (1,)), ((), ())), preferred_element_type=F32)
            ps = []
            for par in range(2):
                sh = jnp.where(valid, s[:, par * 2 * w:(par + 1) * 2 * w], MASK_VALUE)
                sink = jnp.concatenate(
                    [jnp.full((w, 1), sinks_ref[hk * 2 * pairs + 2 * p + par], F32)
                     for p in range(pairs)], axis=0)
                m = jnp.maximum(jnp.max(sh, axis=-1, keepdims=True), sink)
                e = jnp.exp(sh - m)
                denom = jnp.sum(e, axis=-1, keepdims=True) + jnp.exp(sink - m)
                ps.append((e / denom).astype(BF16))
            o = jnp.dot(jnp.concatenate(ps, axis=1), vbd, preferred_element_type=F32)
            for p in range(pairs):
                o_ref[b * w:(b + 1) * w, (hk * pairs + p) * LANES:(hk * pairs + p + 1) * LANES] = \
                    o[p * w:(p + 1) * w, :].astype(o_ref.dtype)


def _swa_attention(qkv, sinks):
    s = qkv.shape[0]
    tq = SWA_TQ
    qcols = SWA_Q_HEADS * SWA_DH
    kvcols = 2 * SWA_KV_HEADS * SWA_DH
    kern = functools.partial(_swa_attn_kernel, tq=tq)
    return pl.pallas_call(
        kern,
        grid=(s // tq,),
        in_specs=[pl.BlockSpec(memory_space=pltpu.SMEM),
                  pl.BlockSpec((tq, qcols), lambda i: (i, 0)),
                  pl.BlockSpec((tq, kvcols), lambda i: (i, qcols // kvcols)),
                  pl.BlockSpec((SWA_WINDOW, kvcols),
                               lambda i: (jnp.maximum(i * (tq // SWA_WINDOW) - 1, 0), qcols // kvcols))],
        out_specs=pl.BlockSpec((tq, qcols), lambda i: (i, 0)),
        out_shape=jax.ShapeDtypeStruct((s, qcols), BF16),
        compiler_params=_params("arbitrary"),
        name="swa_attention",
    )(sinks, qkv, qkv, qkv)


def _pool_kernel(x_ref, xp_ref, g_ref, w_ref, sc_ref, o_ref, ext_ref, wb_ref, *, tm):
    i = pl.program_id(0)

    @pl.when(i == 0)
    def _():
        wb_ref[...] = w_ref[...].astype(BF16)

    g = g_ref[...]
    h = _rms(x_ref[...], g, NORM_EPS)
    hp = _rms(xp_ref[...], g, NORM_EPS)
    ext_ref[0:POOL_HALO, :] = jnp.where(i > 0, hp, 0.0)
    ext_ref[POOL_HALO:, :] = h
    t = i * tm + lax.broadcasted_iota(jnp.int32, (tm, 1), 0)
    for gi, win in enumerate(POOL_WINDOWS):
        cs = slice(gi * POOL_GROUP, (gi + 1) * POOL_GROUP)
        acc = ext_ref[POOL_HALO:POOL_HALO + tm, cs]
        for k in range(1, win):
            acc = acc + ext_ref[POOL_HALO - k:POOL_HALO - k + tm, cs]
        cnt = jnp.minimum(t + 1, win).astype(F32)
        pooled = acc / cnt - ext_ref[POOL_HALO:POOL_HALO + tm, cs]
        y = jnp.dot(pooled.astype(BF16), wb_ref[gi], preferred_element_type=F32)
        o_ref[:, cs] = x_ref[:, cs] + y * sc_ref[:, cs]


def _pool_mixer(x, gains, layer, pool_w, pool_scale, j):
    t, d = x.shape
    tm = POOL_TM
    ng = len(POOL_WINDOWS)
    g3 = gains.reshape(gains.shape[0], 1, d)
    sc3 = pool_scale.reshape(pool_scale.shape[0], 1, d)
    kern = functools.partial(_pool_kernel, tm=tm)
    return pl.pallas_call(
        kern,
        grid=(t // tm,),
        in_specs=[pl.BlockSpec((tm, d), lambda i: (i, 0)),
                  pl.BlockSpec((POOL_HALO, d), lambda i: (jnp.maximum(i * (tm // POOL_HALO) - 1, 0), 0)),
                  pl.BlockSpec((None, 1, d), lambda i: (layer, 0, 0)),
                  pl.BlockSpec((None, ng, POOL_GROUP, POOL_GROUP), lambda i: (j, 0, 0, 0)),
                  pl.BlockSpec((None, 1, d), lambda i: (j, 0, 0))],
        out_specs=pl.BlockSpec((tm, d), lambda i: (i, 0)),
        out_shape=jax.ShapeDtypeStruct((t, d), F32),
        scratch_shapes=[pltpu.VMEM((tm + POOL_HALO, d), F32),
                        pltpu.VMEM((ng, POOL_GROUP, POOL_GROUP), BF16)],
        compiler_params=_params("arbitrary"),
        name="pool_mixer",
    )(x, x, g3, pool_w, sc3)


def _ffn_kernel(start_ref, expert_ref, nsub_ref, nused_ref,
                x_hbm, g_ref, wg_ref, wu_ref, wd_ref, o_hbm,
                xs_ref, acc_ref, xbuf_ref, in_sem, out_sem, *,
                residual):
    s = pl.program_id(0)
    f = pl.program_id(1)
    nf = pl.num_programs(1)
    nsub = nsub_ref[s]
    row0 = start_ref[s] * FFN_SUB

    def in_copy(c, slot):
        return pltpu.make_async_copy(x_hbm.at[pl.ds(row0 + c * FFN_SUB, FFN_SUB)],
                                     xbuf_ref.at[slot], in_sem.at[slot])

    def out_copy(c):
        r = pl.multiple_of(c * FFN_SUB, FFN_SUB)
        return pltpu.make_async_copy(acc_ref.at[pl.ds(r, FFN_SUB)],
                                     o_hbm.at[pl.ds(row0 + c * FFN_SUB, FFN_SUB)], out_sem.at[0])

    @pl.when((nsub > 0) & (f == 0))
    def _():
        in_copy(0, 0).start()

        def load(c, carry):
            slot = c % 2

            @pl.when(c + 1 < nsub)
            def _():
                in_copy(c + 1, 1 - slot).start()

            in_copy(c, slot).wait()
            xr = xbuf_ref[slot]
            r = pl.multiple_of(c * FFN_SUB, FFN_SUB)
            xs_ref[pl.ds(r, FFN_SUB), :] = _rms(xr, g_ref[...], NORM_EPS).astype(BF16)
            acc_ref[pl.ds(r, FFN_SUB), :] = xr if residual else jnp.zeros_like(xr)
            return carry

        lax.fori_loop(0, nsub, load, 0)

    @pl.when(nsub > 0)
    def _():
        def rows_step(first_sub, n_sub):
            rows = pl.ds(pl.multiple_of(first_sub * FFN_SUB, FFN_SUB), n_sub * FFN_SUB)
            xc = xs_ref[rows, :]
            gate = jnp.dot(xc, wg_ref[...].astype(BF16), preferred_element_type=F32)
            up = jnp.dot(xc, wu_ref[...].astype(BF16), preferred_element_type=F32)
            act = (gate * jax.nn.sigmoid(gate) * up).astype(BF16)
            acc_ref[rows, :] += jnp.dot(act, wd_ref[...].astype(BF16), preferred_element_type=F32)

        def chain(cq, carry):
            rows_step(FFN_CHAIN * cq, FFN_CHAIN)
            return carry

        lax.fori_loop(0, nsub // FFN_CHAIN, chain, 0)
        done = nsub // FFN_CHAIN * FFN_CHAIN
        size = FFN_CHAIN // 2
        while size >= 1:
            @pl.when((nsub - done) & size != 0)
            def _(done=done, size=size):
                rows_step(done, size)

            done = done + ((nsub - done) & size)
            size //= 2

    @pl.when((nsub > 0) & (f == nf - 1))
    def _():
        def start(c, carry):
            out_copy(c).start()
            return carry

        def wait(c, carry):
            out_copy(c).wait()
            return carry

        lax.fori_loop(0, nsub, start, 0)
        lax.fori_loop(0, nsub, wait, 0)

    @pl.when((s == pl.num_programs(0) - 1) & (f == nf - 1))
    def _():
        xbuf_ref[0] = jnp.zeros(xbuf_ref.shape[1:], F32)

        def zero_copy(c):
            return pltpu.make_async_copy(xbuf_ref.at[0], o_hbm.at[pl.ds(c * FFN_SUB, FFN_SUB)],
                                         out_sem.at[0])

        def start(c, carry):
            zero_copy(c).start()
            return carry

        def wait(c, carry):
            zero_copy(c).wait()
            return carry

        lax.fori_loop(nused_ref[1], o_hbm.shape[0] // FFN_SUB, start, 0)
        lax.fori_loop(nused_ref[1], o_hbm.shape[0] // FFN_SUB, wait, 0)


def _ffn(x_rows, gains, layer, w_gate, w_up, w_down, sb_start, sb_expert, sb_nsub, n_used, out_rows,
         residual, max_nsub):
    d = x_rows.shape[1]
    dff = w_gate.shape[2]
    nsb = sb_start.shape[0]
    nf = dff // FFN_TF
    rows = FFN_SUB * max_nsub
    g3 = gains.reshape(gains.shape[0], 1, d)

    def w_in(s, f, st, ex, ns, nu):
        return (ex[s], 0, jnp.where(s < nu[0], f, nf - 1))

    def w_out(s, f, st, ex, ns, nu):
        return (ex[s], jnp.where(s < nu[0], f, nf - 1), 0)

    grid_spec = pltpu.PrefetchScalarGridSpec(
        num_scalar_prefetch=4,
        grid=(nsb, nf),
        in_specs=[pl.BlockSpec(memory_space=pl.ANY),
                  pl.BlockSpec((None, 1, d), lambda s, f, *_: (layer, 0, 0)),
                  pl.BlockSpec((None, d, FFN_TF), w_in),
                  pl.BlockSpec((None, d, FFN_TF), w_in),
                  pl.BlockSpec((None, FFN_TF, d), w_out)],
        out_specs=pl.BlockSpec(memory_space=pl.ANY),
        scratch_shapes=[pltpu.VMEM((rows, d), BF16),
                        pltpu.VMEM((rows, d), F32),
                        pltpu.VMEM((2, FFN_SUB, d), F32),
                        pltpu.SemaphoreType.DMA((2,)),
                        pltpu.SemaphoreType.DMA((1,))],
    )
    kern = functools.partial(_ffn_kernel, residual=residual)
    return pl.pallas_call(
        kern,
        grid_spec=grid_spec,
        out_shape=jax.ShapeDtypeStruct((out_rows, d), F32),
        compiler_params=pltpu.CompilerParams(dimension_semantics=("arbitrary", "arbitrary"),
                                             vmem_limit_bytes=60 * 1024 * 1024),
        name="ffn_residual" if residual else "ffn_experts",
    )(sb_start, sb_expert, sb_nsub, n_used, x_rows, g3, w_gate, w_up, w_down)


def _dense_ffn(x, gains, layer, w_gate, w_up, w_down, m):
    t = x.shape[0]
    total_sub = t // FFN_SUB
    per_sb = 2 * FFN_CHAIN
    nsb = -(-total_sub // per_sb)
    start = np.arange(nsb, dtype=np.int32) * per_sb
    nsub = np.minimum(per_sb, total_sub - start).astype(np.int32)
    return _ffn(x, gains, layer, w_gate, w_up, w_down,
                jnp.asarray(start), jnp.full((nsb,), m, jnp.int32), jnp.asarray(nsub),
                jnp.asarray([nsb, total_sub], jnp.int32), t, residual=True, max_nsub=per_sb)


def _router_kernel(x_ref, g_ref, wr_ref, idx_ref, gate_ref, cnt_ref, carry_ref, *, tm):
    @pl.when(pl.program_id(0) == 0)
    def _():
        carry_ref[...] = jnp.zeros(carry_ref.shape, F32)

    h = _rms(x_ref[...], g_ref[...], NORM_EPS)
    logit = [jnp.sum(h * wr_ref[e:e + 1, :], axis=-1, keepdims=True) for e in range(N_EXPERTS)]

    def top(vals):
        best = vals[0]
        for v in vals[1:]:
            best = jnp.maximum(best, v)
        idx = jnp.full(best.shape, N_EXPERTS, jnp.int32)
        for e in reversed(range(N_EXPERTS)):
            idx = jnp.where(vals[e] == best, e, idx)
        return best, idx

    m1, i1 = top(logit)
    m2, i2 = top([jnp.where(i1 == e, -jnp.inf, logit[e]) for e in range(N_EXPERTS)])
    d = jnp.exp(m2 - m1)
    g1 = 1.0 / (1.0 + d)
    g2 = d / (1.0 + d)

    lane = lax.broadcasted_iota(jnp.int32, (tm, LANES), 1)
    onehot = ((lane == i1) | (lane == i2)).astype(BF16)
    r = lax.broadcasted_iota(jnp.int32, (tm, tm), 0)
    c = lax.broadcasted_iota(jnp.int32, (tm, tm), 1)
    before = (c < r).astype(BF16)
    rank = jnp.dot(before, onehot, preferred_element_type=F32) + carry_ref[...]
    r1 = jnp.sum(jnp.where(lane == i1, rank, 0.0), axis=-1, keepdims=True).astype(jnp.int32)
    r2 = jnp.sum(jnp.where(lane == i2, rank, 0.0), axis=-1, keepdims=True).astype(jnp.int32)
    carry_ref[...] += jnp.sum(onehot.astype(F32), axis=0, keepdims=True)
    cnt_ref[...] = carry_ref[...]

    l8 = lax.broadcasted_iota(jnp.int32, (tm, 8), 1)
    idx_ref[...] = jnp.where(l8 == 0, i1, jnp.where(l8 == 1, i2, jnp.where(l8 == 2, r1, r2)))
    gate_ref[...] = jnp.where(l8 == 0, g1, g2)


def _router(x, gains, layer, router_t, m):
    t, d = x.shape
    tm = ROUTER_TM
    g3 = gains.reshape(gains.shape[0], 1, d)
    kern = functools.partial(_router_kernel, tm=tm)
    return pl.pallas_call(
        kern,
        grid=(t // tm,),
        in_specs=[pl.BlockSpec((tm, d), lambda i: (i, 0)),
                  pl.BlockSpec((None, 1, d), lambda i: (layer, 0, 0)),
                  pl.BlockSpec((None, N_EXPERTS, d), lambda i: (m, 0, 0))],
        out_specs=[pl.BlockSpec((tm, 8), lambda i: (i, 0)),
                   pl.BlockSpec((tm, 8), lambda i: (i, 0)),
                   pl.BlockSpec((1, LANES), lambda i: (0, 0))],
        out_shape=[jax.ShapeDtypeStruct((t, 8), jnp.int32),
                   jax.ShapeDtypeStruct((t, 8), F32),
                   jax.ShapeDtypeStruct((1, LANES), F32)],
        scratch_shapes=[pltpu.VMEM((1, LANES), F32)],
        compiler_params=_params("arbitrary"),
        name="router",
    )(x, g3, router_t)


def _scatter_kernel(d0_ref, d1_ref, pad_ref, x_ref, o_hbm, zero_ref, sem, *, ct):
    base = pl.program_id(0) * ct

    @pl.when(pl.program_id(0) == 0)
    def _():
        zero_ref[...] = jnp.zeros(zero_ref.shape, F32)
        total_sub = o_hbm.shape[0] // FFN_SUB

        def fill(row, nrows):
            return pltpu.make_async_copy(zero_ref.at[pl.ds(0, nrows)],
                                         o_hbm.at[pl.ds(pl.multiple_of(row, 8), nrows)], sem.at[1])

        def window(e):
            return fill(pad_ref[e] // 8 * 8, zero_ref.shape[0])

        def tail_start(c, carry):
            fill(c * FFN_SUB, FFN_SUB).start()
            return carry

        def tail_wait(c, carry):
            fill(c * FFN_SUB, FFN_SUB).wait()
            return carry

        for e in range(N_EXPERTS):
            window(e).start()
            window(e).wait()
        lax.fori_loop(pad_ref[N_EXPERTS], total_sub, tail_start, 0)
        lax.fori_loop(pad_ref[N_EXPERTS], total_sub, tail_wait, 0)

    def issue(t, carry):
        tok = base + t
        pltpu.make_async_copy(x_ref.at[pl.ds(t, 1)], o_hbm.at[pl.ds(d0_ref[tok], 1)], sem.at[0]).start()
        pltpu.make_async_copy(x_ref.at[pl.ds(t, 1)], o_hbm.at[pl.ds(d1_ref[tok], 1)],
                              sem.at[0]).start(priority=1)
        return carry

    def wait(t, carry):
        pltpu.make_async_copy(x_ref.at[pl.ds(0, 1)], o_hbm.at[pl.ds(0, 1)], sem.at[0]).wait()
        return carry

    lax.fori_loop(0, ct, issue, 0)
    lax.fori_loop(0, 2 * ct, wait, 0)


def _scatter_rows(x, d0, d1, pad, n_rows):
    t, d = x.shape
    ct = SCATTER_CT
    kern = functools.partial(_scatter_kernel, ct=ct)
    grid_spec = pltpu.PrefetchScalarGridSpec(
        num_scalar_prefetch=3,
        grid=(t // ct,),
        in_specs=[pl.BlockSpec((ct, d), lambda i, *_: (i, 0))],
        out_specs=pl.BlockSpec(memory_space=pl.ANY),
        scratch_shapes=[pltpu.VMEM((FFN_SUB + 8, d), F32), pltpu.SemaphoreType.DMA((2,))],
    )
    return pl.pallas_call(
        kern,
        grid_spec=grid_spec,
        out_shape=jax.ShapeDtypeStruct((n_rows + 2 * FFN_SUB, d), F32),
        compiler_params=_params("arbitrary"),
        name="scatter_rows",
    )(d0, d1, pad, x)


def _combine_kernel(d0_ref, d1_ref, x_ref, gate_ref, g_ref, y_hbm, *rest, ct, with_x):
    if with_x:
        o_ref, h_ref, buf_ref, sem = rest
    else:
        h_ref, buf_ref, sem = rest
    step = pl.program_id(0)
    nsteps = pl.num_programs(0)

    def row_copy(d_ref, k, slot, tok, t):
        return pltpu.make_async_copy(y_hbm.at[pl.ds(d_ref[tok], 1)],
                                     buf_ref.at[slot, k, pl.ds(t, 1)], sem.at[slot])

    def issue_all(for_step, slot):
        def issue(t, carry):
            tok = for_step * ct + t
            row_copy(d0_ref, 0, slot, tok, t).start()
            row_copy(d1_ref, 1, slot, tok, t).start(priority=1)
            return carry

        lax.fori_loop(0, ct, issue, 0)

    slot = step % 2

    @pl.when(step == 0)
    def _():
        issue_all(0, 0)

    @pl.when(step + 1 < nsteps)
    def _():
        issue_all(step + 1, 1 - slot)

    def wait(t, carry):
        row_copy(d0_ref, 0, slot, step * ct, 0).wait()
        row_copy(d1_ref, 1, slot, step * ct, 0).wait()
        return carry

    lax.fori_loop(0, ct, wait, 0)
    gts = gate_ref[...]
    y = x_ref[...] + (buf_ref[slot, 0] * gts[:, 0:1] + buf_ref[slot, 1] * gts[:, 1:2])
    if with_x:
        o_ref[...] = y
    h_ref[...] = _rms(y, g_ref[...], NORM_EPS).astype(h_ref.dtype)


def _combine(x, y_rows, gates, d0, d1, next_gains, next_layer, h_dtype, with_x):
    t, d = x.shape
    ct = COMBINE_CT
    kern = functools.partial(_combine_kernel, ct=ct, with_x=with_x)
    g3 = next_gains.reshape(next_gains.shape[0], 1, d)
    row_spec = pl.BlockSpec((ct, d), lambda i, *_: (i, 0))
    grid_spec = pltpu.PrefetchScalarGridSpec(
        num_scalar_prefetch=2,
        grid=(t // ct,),
        in_specs=[row_spec,
                  pl.BlockSpec((ct, 8), lambda i, *_: (i, 0)),
                  pl.BlockSpec((None, 1, d), lambda i, *_: (next_layer, 0, 0)),
                  pl.BlockSpec(memory_space=pl.ANY)],
        out_specs=[row_spec, row_spec] if with_x else [row_spec],
        scratch_shapes=[pltpu.VMEM((2, 2, ct, d), F32), pltpu.SemaphoreType.DMA((2,))],
    )
    h_shape = jax.ShapeDtypeStruct((t, d), h_dtype)
    return pl.pallas_call(
        kern,
        grid_spec=grid_spec,
        out_shape=[jax.ShapeDtypeStruct((t, d), F32), h_shape] if with_x else [h_shape],
        compiler_params=_params("arbitrary"),
        name="moe_combine",
    )(d0, d1, x, gates, g3, y_rows)


def _moe_ffn(x, gains, layer, router_w, w_gate, w_up, w_down, m, next_gains, next_layer, h_dtype,
             with_x):
    t, d = x.shape
    router_t = jnp.swapaxes(router_w, 1, 2)
    idx, gates, counts = _router(x, gains, layer, router_t, m)

    cnt = counts[0, :N_EXPERTS].astype(jnp.int32)
    nsub_e = (cnt + FFN_SUB - 1) // FFN_SUB
    start_e = jnp.cumsum(nsub_e) - nsub_e
    row_start = start_e * FFN_SUB
    d0 = row_start[idx[:, 0]] + idx[:, 2]
    d1 = row_start[idx[:, 1]] + idx[:, 3]

    max_sub = t * 2 // FFN_SUB + N_EXPERTS
    n_rows = max_sub * FFN_SUB
    nsb = -(-(t * 2 // FFN_SUB) // FFN_NSUB) + N_EXPERTS
    sbs_e = (nsub_e + FFN_NSUB - 1) // FFN_NSUB
    cum = jnp.cumsum(sbs_e)
    n_used = cum[-1]
    slot = jnp.arange(nsb, dtype=jnp.int32)
    e_of = jnp.minimum(jnp.sum((slot[:, None] >= cum[None, :]).astype(jnp.int32), axis=1), N_EXPERTS - 1)
    k_of = slot - (cum - sbs_e)[e_of]
    used = slot < n_used
    last_e = e_of[jnp.maximum(n_used - 1, 0)]
    sb_start = jnp.where(used, start_e[e_of] + k_of * FFN_NSUB, 0).astype(jnp.int32)
    sb_nsub = jnp.where(used, jnp.clip(nsub_e[e_of] - k_of * FFN_NSUB, 0, FFN_NSUB), 0).astype(jnp.int32)
    sb_expert = (m * N_EXPERTS + jnp.where(used, e_of, last_e)).astype(jnp.int32)

    flat = lambda w: w.reshape((-1,) + w.shape[2:])
    pad = jnp.concatenate([row_start + cnt, jnp.sum(nsub_e)[None]]).astype(jnp.int32)
    x_rows = _scatter_rows(x, d0, d1, pad, n_rows)
    y_rows = _ffn(x_rows, gains, layer, flat(w_gate), flat(w_up), flat(w_down),
                  sb_start, sb_expert, sb_nsub,
                  jnp.stack([n_used, jnp.sum(nsub_e)]).astype(jnp.int32), n_rows, residual=False,
                  max_nsub=FFN_NSUB)
    return _combine(x, y_rows, gates, d0, d1, next_gains, next_layer, h_dtype, with_x)


def kernel(x, positions, norm1, norm2, final_norm, diff_w_qkv, diff_lambda_q1, diff_lambda_k1, diff_lambda_q2, diff_lambda_k2, diff_subln, diff_w_o, pool_w, pool_scale, swa_w_qkv, swa_b_qkv, swa_sinks, swa_w_o, swa_b_o, ffn_w_gate, ffn_w_up, ffn_w_down, moe_router, moe_w_gate, moe_w_up, moe_w_down):
    b, s, d = x.shape
    assert b == 1 and d == D_MODEL
    xs = x.reshape(s, d)
    pos_col = positions.reshape(s, 1)
    cos_a, sin_a = _rope_tables(pos_col, DIFF_DH)
    cos_c, sin_c = _rope_tables(pos_col, SWA_DH)
    zero_bias_qkv = jnp.zeros((1, diff_w_qkv.shape[2]), F32)
    zero_bias_o = jnp.zeros((1, d), F32)

    h = None
    for i in range(N_LAYERS):
        kind, j = i % 3, i // 3
        if kind != 1 and h is None:
            h = _norm(xs, norm1, i, BF16)
        if kind == 0:
            lambda_init = 0.8 - 0.6 * math.exp(-0.3 * i)
            qkv = _mm_qkv(h, diff_w_qkv, j, zero_bias_qkv, cos_a, sin_a, half=DIFF_DH // 2,
                          scale=DIFF_DH ** -0.5 * math.log2(math.e), q_cols=2 * DIFF_HEADS * DIFF_DH,
                          k_cols=2 * DIFF_HEADS * DIFF_DH)
            att = _diff_attention(qkv, diff_lambda_q1, diff_lambda_k1, diff_lambda_q2, diff_lambda_k2,
                                  diff_subln, j, lambda_init)
            xs = _mm_residual(att, diff_w_o, j, zero_bias_o, xs)
        elif kind == 1:
            xs = _pool_mixer(xs, norm1, i, pool_w, pool_scale, j)
        else:
            qkv = _mm_qkv(h, swa_w_qkv, j, swa_b_qkv[j][None, :], cos_c, sin_c, half=SWA_DH // 2,
                          scale=SWA_DH ** -0.5, q_cols=SWA_Q_HEADS * SWA_DH,
                          k_cols=SWA_KV_HEADS * SWA_DH)
            att = _swa_attention(qkv, swa_sinks[j])
            xs = _mm_residual(att, swa_w_o, j, swa_b_o[j][None, :], xs)
        m = i // 2
        h = None
        if i % 2 == 0:
            xs = _dense_ffn(xs, norm2, i, ffn_w_gate, ffn_w_up, ffn_w_down, m)
        elif i + 1 < N_LAYERS:
            xs, h = _moe_ffn(xs, norm2, i, moe_router, moe_w_gate, moe_w_up, moe_w_down, m,
                             norm1, i + 1, BF16, True)
        else:
            (out,) = _moe_ffn(xs, norm2, i, moe_router, moe_w_gate, moe_w_up, moe_w_down, m,
                              final_norm[None, :], 0, F32, False)
            return out.reshape(b, s, d)
    return _norm(xs, final_norm[None, :], 0, F32).reshape(b, s, d)
```

```python
import functools
import math

import numpy as np
import jax
import jax.numpy as jnp
from jax import lax
from jax.experimental import pallas as pl
from jax.experimental.pallas import tpu as pltpu

F32 = jnp.float32
BF16 = jnp.bfloat16

D_MODEL = 2048
N_LAYERS = 4
NORM_EPS = 1e-6
SUBLN_EPS = 1e-5
MASK_VALUE = -1e30
ROPE_THETA = 10000.0

DIFF_HEADS = 8
DIFF_DH = 128
SWA_DH = 64
SWA_Q_HEADS = 32
SWA_KV_HEADS = 4
SWA_WINDOW = 128
POOL_WINDOWS = (2, 4, 8, 16)
POOL_GROUP = 512
POOL_HALO = 16
N_EXPERTS = 8

LANES = 128
VMEM_LIMIT = 56 * 1024 * 1024

MM_TM = 2048
MM_TN = 512
MM_CHUNK = 256
NORM_TM = 512
ATT_TQ = 2048
ATT_TK = 512
ATT_QC = 256
SWA_TQ = 256
POOL_TM = 512
FFN_SUB = 256
FFN_NSUB = 9
FFN_CHAIN = 4
FFN_TF = 256
ROUTER_TM = 256
SCATTER_CT = 512
COMBINE_CT = 256


def _params(*sem):
    return pltpu.CompilerParams(dimension_semantics=sem, vmem_limit_bytes=VMEM_LIMIT)


def _rms(x, g, eps):
    return x * lax.rsqrt(jnp.mean(x * x, axis=-1, keepdims=True) + eps) * g


def _rope_table_kernel(pos_ref, inv_ref, sgn_ref, cos_ref, sin_ref):
    ang = pos_ref[...].astype(F32) * inv_ref[...]
    cos_ref[...] = jnp.cos(ang)
    sin_ref[...] = jnp.sin(ang) * sgn_ref[...]


def _rope_tables(pos_col, dh):
    s = pos_col.shape[0]
    half = dh // 2
    lane = np.arange(LANES)
    inv = (ROPE_THETA ** (-(lane % half).astype(np.float32) * 2.0 / dh)).astype(np.float32)
    sgn = np.where((lane % dh) < half, -1.0, 1.0).astype(np.float32)
    tm = 1024
    return pl.pallas_call(
        _rope_table_kernel,
        grid=(s // tm,),
        in_specs=[pl.BlockSpec((tm, 1), lambda i: (i, 0)),
                  pl.BlockSpec((1, LANES), lambda i: (0, 0)),
                  pl.BlockSpec((1, LANES), lambda i: (0, 0))],
        out_specs=[pl.BlockSpec((tm, LANES), lambda i: (i, 0))] * 2,
        out_shape=[jax.ShapeDtypeStruct((s, LANES), F32)] * 2,
        compiler_params=_params("arbitrary"),
        name="rope_tables",
    )(pos_col, jnp.asarray(inv)[None, :], jnp.asarray(sgn)[None, :])


def _swap_halves(x, half):
    if 2 * half == LANES:
        return pltpu.roll(x, half, 1)
    lane = lax.broadcasted_iota(jnp.int32, x.shape, 1)
    first = (lane % (2 * half)) < half
    return jnp.where(first, pltpu.roll(x, LANES - half, 1), pltpu.roll(x, half, 1))


def _norm_kernel(x_ref, g_ref, o_ref):
    o_ref[...] = _rms(x_ref[...], g_ref[...], NORM_EPS).astype(o_ref.dtype)


def _norm(x, gains, layer, out_dtype):
    t, d = x.shape
    g3 = gains.reshape(gains.shape[0], 1, d)
    return pl.pallas_call(
        _norm_kernel,
        grid=(t // NORM_TM,),
        in_specs=[pl.BlockSpec((NORM_TM, d), lambda i: (i, 0)),
                  pl.BlockSpec((None, 1, d), lambda i: (layer, 0, 0))],
        out_specs=pl.BlockSpec((NORM_TM, d), lambda i: (i, 0)),
        out_shape=jax.ShapeDtypeStruct((t, d), out_dtype),
        compiler_params=_params("arbitrary"),
        name="rmsnorm",
    )(x, g3)


def _mm_qkv_kernel(a_ref, w_ref, b_ref, cos_ref, sin_ref, o_ref, wb_ref, *,
                   half, scale, q_cols, k_cols):
    tn = o_ref.shape[1]
    col0 = pl.program_id(0) * tn

    @pl.when(pl.program_id(1) == 0)
    def _():
        wb_ref[...] = w_ref[...].astype(BF16)

    cos = cos_ref[...]
    sin = sin_ref[...]
    for c in range(tn // MM_CHUNK):
        cs = slice(c * MM_CHUNK, (c + 1) * MM_CHUNK)
        col = col0 + c * MM_CHUNK
        rot = jnp.where(col < q_cols + k_cols, 1.0, 0.0)
        sc = jnp.where(col < q_cols, scale, 1.0)
        cf = (1.0 + rot * (cos - 1.0)) * sc
        sf = sin * (rot * sc)
        acc = jnp.dot(a_ref[...], wb_ref[:, cs], preferred_element_type=F32) + b_ref[:, cs]
        out = []
        for g in range(MM_CHUNK // LANES):
            xg = acc[:, g * LANES:(g + 1) * LANES]
            out.append(xg * cf + _swap_halves(xg, half) * sf)
        o_ref[:, cs] = jnp.concatenate(out, axis=1).astype(o_ref.dtype)


def _mm_qkv(a, w, layer, bias, cos, sin, *, half, scale, q_cols, k_cols):
    m, k = a.shape
    n = w.shape[2]
    tm, tn = MM_TM, MM_TN
    assert q_cols % MM_CHUNK == 0 and k_cols % MM_CHUNK == 0 and tn % MM_CHUNK == 0
    kern = functools.partial(_mm_qkv_kernel, half=half, scale=scale, q_cols=q_cols, k_cols=k_cols)
    return pl.pallas_call(
        kern,
        grid=(n // tn, m // tm),
        in_specs=[pl.BlockSpec((tm, k), lambda j, i: (i, 0)),
                  pl.BlockSpec((None, k, tn), lambda j, i: (layer, 0, j)),
                  pl.BlockSpec((1, tn), lambda j, i: (0, j)),
                  pl.BlockSpec((tm, LANES), lambda j, i: (i, 0)),
                  pl.BlockSpec((tm, LANES), lambda j, i: (i, 0))],
        out_specs=pl.BlockSpec((tm, tn), lambda j, i: (i, j)),
        out_shape=jax.ShapeDtypeStruct((m, n), BF16),
        scratch_shapes=[pltpu.VMEM((k, tn), BF16)],
        compiler_params=_params("arbitrary", "arbitrary"),
        name="qkv_proj",
    )(a, w, bias, cos, sin)


def _mm_res_kernel(a_ref, w_ref, b_ref, x_ref, o_ref, wb_ref):
    @pl.when(pl.program_id(1) == 0)
    def _():
        wb_ref[...] = w_ref[...].astype(BF16)

    acc = jnp.dot(a_ref[...], wb_ref[...], preferred_element_type=F32)
    o_ref[...] = x_ref[...] + (acc + b_ref[...])


def _mm_residual(a, w, layer, bias, x):
    m, k = a.shape
    n = w.shape[2]
    tm, tn = MM_TM, MM_TN
    return pl.pallas_call(
        _mm_res_kernel,
        grid=(n // tn, m // tm),
        in_specs=[pl.BlockSpec((tm, k), lambda j, i: (i, 0)),
                  pl.BlockSpec((None, k, tn), lambda j, i: (layer, 0, j)),
                  pl.BlockSpec((1, tn), lambda j, i: (0, j)),
                  pl.BlockSpec((tm, tn), lambda j, i: (i, j))],
        out_specs=pl.BlockSpec((tm, tn), lambda j, i: (i, j)),
        out_shape=jax.ShapeDtypeStruct((m, n), F32),
        scratch_shapes=[pltpu.VMEM((k, tn), BF16)],
        compiler_params=_params("arbitrary", "arbitrary"),
        name="out_proj",
    )(a, w, bias, x)


def _diff_attn_kernel(lq1_ref, lk1_ref, lq2_ref, lk2_ref, sub_ref,
                      q1_ref, q2_ref, k1_ref, k2_ref, v_ref, o_ref,
                      vt_ref, m1_ref, l1_ref, a1_ref, m2_ref, l2_ref, a2_ref, *, tq, tk, lambda_init):
    i = pl.program_id(1)
    qc = ATT_QC

    @pl.when(i == 0)
    def _():
        def transpose_tile(c, carry):
            rows = pl.ds(pl.multiple_of(c * tk, tk), tk)
            vt_ref[c] = v_ref[rows, :].astype(F32).T.astype(BF16)
            return carry

        lax.fori_loop(0, vt_ref.shape[0], transpose_tile, 0)

    for m_ref, l_ref, a_ref in ((m1_ref, l1_ref, a1_ref), (m2_ref, l2_ref, a2_ref)):
        m_ref[...] = jnp.full(m_ref.shape, MASK_VALUE, F32)
        l_ref[...] = jnp.zeros(l_ref.shape, F32)
        a_ref[...] = jnp.zeros(a_ref.shape, F32)
    q1t = q1_ref[...].astype(F32).T.astype(BF16)
    q2t = q2_ref[...].astype(F32).T.astype(BF16)

    def step(j, diag):
        ks = pl.ds(pl.multiple_of(j * tk, tk), tk)
        vt = vt_ref[j]
        for qt, k_ref, m_ref, l_ref, a_ref in ((q1t, k1_ref, m1_ref, l1_ref, a1_ref),
                                               (q2t, k2_ref, m2_ref, l2_ref, a2_ref)):
            kblk = k_ref[ks, :]
            for c in range(tq // qc):
                if diag is not None and (c + 1) * qc - 1 < diag * tk:
                    continue
                cs = slice(c * qc, (c + 1) * qc)
                s = jnp.dot(kblk, qt[:, cs], preferred_element_type=F32)
                if diag is not None and c * qc < (diag + 1) * tk - 1:
                    key = lax.broadcasted_iota(jnp.int32, s.shape, 0) + diag * tk
                    qry = lax.broadcasted_iota(jnp.int32, s.shape, 1) + c * qc
                    s = jnp.where(key <= qry, s, MASK_VALUE)
                m_old = m_ref[:, cs]
                m_new = jnp.maximum(m_old, jnp.max(s, axis=0, keepdims=True))
                p = jnp.exp2(s - m_new)
                alpha = jnp.exp2(m_old - m_new)
                l_ref[:, cs] = alpha * l_ref[:, cs] + jnp.sum(p, axis=0, keepdims=True)
                a_ref[:, cs] = alpha * a_ref[:, cs] + jnp.dot(vt, p.astype(BF16),
                                                              preferred_element_type=F32)
                m_ref[:, cs] = m_new

    def full_step(j, carry):
        step(j, None)
        return carry

    per_tile = tq // tk
    lax.fori_loop(0, i * per_tile, full_step, 0)
    for d in range(per_tile):
        step(i * per_tile + d, d)

    lam = (jnp.exp(jnp.sum(lq1_ref[...] * lk1_ref[...], axis=-1, keepdims=True))
           - jnp.exp(jnp.sum(lq2_ref[...] * lk2_ref[...], axis=-1, keepdims=True))
           + lambda_init)
    o = a1_ref[...] / l1_ref[...] - lam * (a2_ref[...] / l2_ref[...])
    o = o * lax.rsqrt(jnp.mean(o * o, axis=0, keepdims=True) + SUBLN_EPS) * sub_ref[...]
    o_ref[...] = (o * (1.0 - lambda_init)).T.astype(o_ref.dtype)


def _diff_attention(qkv, lq1, lk1, lq2, lk2, subln, layer, lambda_init):
    s = qkv.shape[0]
    tq = ATT_TQ
    dh, dv = DIFF_DH, 2 * DIFF_DH
    k_off = (2 * DIFF_HEADS * dh) // dh
    v_off = (4 * DIFF_HEADS * dh) // dv
    vec = lambda a: a.reshape(a.shape[0], 1, a.shape[1])
    lam_spec = pl.BlockSpec((None, 1, dh), lambda h, i: (layer, 0, 0))
    tk = ATT_TK
    kern = functools.partial(_diff_attn_kernel, tq=tq, tk=tk, lambda_init=lambda_init)
    return pl.pallas_call(
        kern,
        grid=(DIFF_HEADS, s // tq),
        in_specs=[lam_spec, lam_spec, lam_spec, lam_spec,
                  pl.BlockSpec((None, dv, 1), lambda h, i: (layer, 0, 0)),
                  pl.BlockSpec((tq, dh), lambda h, i: (i, 2 * h)),
                  pl.BlockSpec((tq, dh), lambda h, i: (i, 2 * h + 1)),
                  pl.BlockSpec((s, dh), lambda h, i: (0, k_off + 2 * h)),
                  pl.BlockSpec((s, dh), lambda h, i: (0, k_off + 2 * h + 1)),
                  pl.BlockSpec((s, dv), lambda h, i: (0, v_off + h))],
        out_specs=pl.BlockSpec((tq, dv), lambda h, i: (i, h)),
        out_shape=jax.ShapeDtypeStruct((s, DIFF_HEADS * dv), BF16),
        scratch_shapes=[pltpu.VMEM((s // tk, dv, tk), BF16),
                        pltpu.VMEM((1, tq), F32), pltpu.VMEM((1, tq), F32), pltpu.VMEM((dv, tq), F32),
                        pltpu.VMEM((1, tq), F32), pltpu.VMEM((1, tq), F32), pltpu.VMEM((dv, tq), F32)],
        compiler_params=_params("arbitrary", "arbitrary"),
        name="diff_attention",
    )(vec(lq1), vec(lk1), vec(lq2), vec(lk2), subln[:, :, None], qkv, qkv, qkv, qkv, qkv)


def _swa_attn_kernel(sinks_ref, q_ref, kv_ref, kvp_ref, o_ref, *, tq):
    w = SWA_WINDOW
    dh = SWA_DH
    pairs = SWA_Q_HEADS // SWA_KV_HEADS // 2
    tile = pl.program_id(0)
    kcols = SWA_KV_HEADS * dh

    key = lax.broadcasted_iota(jnp.int32, (2 * w, pairs * w), 0)
    qry = lax.broadcasted_iota(jnp.int32, (2 * w, pairs * w), 1) % w
    band = (key > qry) & (key <= qry + w)
    pair_of_lane = lax.broadcasted_iota(jnp.int32, (1, pairs * w), 1) // w
    lane = lax.broadcasted_iota(jnp.int32, (2 * w, LANES), 1)
    zeros = jnp.zeros((2 * w, LANES), BF16)

    for b in range(tq // w):
        if b == 0:
            prev, first_key = kvp_ref[...], jnp.where(tile > 0, 0, w)
        else:
            prev, first_key = kv_ref[(b - 1) * w:b * w, :], 0
        kvb = jnp.concatenate([prev, kv_ref[b * w:(b + 1) * w, :]], axis=0)
        valid = band & (key >= first_key)
        for hk in range(SWA_KV_HEADS):
            c0 = (hk // 2) * LANES
            kc = kvb[:, c0:c0 + LANES]
            vc = kvb[:, kcols + c0:kcols + c0 + LANES]
            own = (lane // dh) == (hk % 2)
            k_own = jnp.where(own, kc, zeros)
            v_own = jnp.where(own, vc, zeros)
            k_sw = pltpu.roll(k_own, dh, 1)
            v_sw = pltpu.roll(v_own, dh, 1)
            if hk % 2 == 0:
                k_even, k_odd, v_even, v_odd = k_own, k_sw, v_own, v_sw
            else:
                k_even, k_odd, v_even, v_odd = k_sw, k_own, v_sw, v_own
            kbd = jnp.concatenate([k_even, k_odd], axis=0)
            vbd_t = jnp.concatenate([v_even, v_odd], axis=0).astype(F32).T.astype(BF16)
            qp_t = jnp.concatenate(
                [q_ref[b * w:(b + 1) * w,
                       (hk * pairs + p) * LANES:(hk * pairs + p + 1) * LANES].astype(F32).T
                 for p in range(pairs)], axis=1).astype(BF16)
            s = jnp.dot(kbd, qp_t, preferred_element_type=F32)
            ps = []
            for par in range(2):
                sh = jnp.where(valid, s[par * 2 * w:(par + 1) * 2 * w, :], MASK_VALUE)
                sink = jnp.zeros((1, pairs * w), F32)
                for p in range(pairs):
                    sink = jnp.where(pair_of_lane == p, sinks_ref[hk * 2 * pairs + 2 * p + par], sink)
                m = jnp.maximum(jnp.max(sh, axis=0, keepdims=True), sink)
                e = jnp.exp(sh - m)
                denom = jnp.sum(e, axis=0, keepdims=True) + jnp.exp(sink - m)
                ps.append((e / denom).astype(BF16))
            o_t = jnp.dot(vbd_t, jnp.concatenate(ps, axis=0), preferred_element_type=F32)
            for p in range(pairs):
                o_ref[b * w:(b + 1) * w, (hk * pairs + p) * LANES:(hk * pairs + p + 1) * LANES] = \
                    o_t[:, p * w:(p + 1) * w].T.astype(o_ref.dtype)


def _swa_attention(qkv, sinks):
    s = qkv.shape[0]
    tq = SWA_TQ
    qcols = SWA_Q_HEADS * SWA_DH
    kvcols = 2 * SWA_KV_HEADS * SWA_DH
    kern = functools.partial(_swa_attn_kernel, tq=tq)
    return pl.pallas_call(
        kern,
        grid=(s // tq,),
        in_specs=[pl.BlockSpec(memory_space=pltpu.SMEM),
                  pl.BlockSpec((tq, qcols), lambda i: (i, 0)),
                  pl.BlockSpec((tq, kvcols), lambda i: (i, qcols // kvcols)),
                  pl.BlockSpec((SWA_WINDOW, kvcols),
                               lambda i: (jnp.maximum(i * (tq // SWA_WINDOW) - 1, 0), qcols // kvcols))],
        out_specs=pl.BlockSpec((tq, qcols), lambda i: (i, 0)),
        out_shape=jax.ShapeDtypeStruct((s, qcols), BF16),
        compiler_params=_params("arbitrary"),
        name="swa_attention",
    )(sinks, qkv, qkv, qkv)


def _pool_kernel(x_ref, xp_ref, g_ref, w_ref, sc_ref, o_ref, ext_ref, wb_ref, *, tm):
    i = pl.program_id(0)

    @pl.when(i == 0)
    def _():
        wb_ref[...] = w_ref[...].astype(BF16)

    g = g_ref[...]
    h = _rms(x_ref[...], g, NORM_EPS)
    hp = _rms(xp_ref[...], g, NORM_EPS)
    ext_ref[0:POOL_HALO, :] = jnp.where(i > 0, hp, 0.0)
    ext_ref[POOL_HALO:, :] = h
    t = i * tm + lax.broadcasted_iota(jnp.int32, (tm, 1), 0)
    for gi, win in enumerate(POOL_WINDOWS):
        cs = slice(gi * POOL_GROUP, (gi + 1) * POOL_GROUP)
        acc = ext_ref[POOL_HALO:POOL_HALO + tm, cs]
        for k in range(1, win):
            acc = acc + ext_ref[POOL_HALO - k:POOL_HALO - k + tm, cs]
        cnt = jnp.minimum(t + 1, win).astype(F32)
        pooled = acc / cnt - ext_ref[POOL_HALO:POOL_HALO + tm, cs]
        y = jnp.dot(pooled.astype(BF16), wb_ref[gi], preferred_element_type=F32)
        o_ref[:, cs] = x_ref[:, cs] + y * sc_ref[:, cs]


def _pool_mixer(x, gains, layer, pool_w, pool_scale, j):
    t, d = x.shape
    tm = POOL_TM
    ng = len(POOL_WINDOWS)
    g3 = gains.reshape(gains.shape[0], 1, d)
    sc3 = pool_scale.reshape(pool_scale.shape[0], 1, d)
    kern = functools.partial(_pool_kernel, tm=tm)
    return pl.pallas_call(
        kern,
        grid=(t // tm,),
        in_specs=[pl.BlockSpec((tm, d), lambda i: (i, 0)),
                  pl.BlockSpec((POOL_HALO, d), lambda i: (jnp.maximum(i * (tm // POOL_HALO) - 1, 0), 0)),
                  pl.BlockSpec((None, 1, d), lambda i: (layer, 0, 0)),
                  pl.BlockSpec((None, ng, POOL_GROUP, POOL_GROUP), lambda i: (j, 0, 0, 0)),
                  pl.BlockSpec((None, 1, d), lambda i: (j, 0, 0))],
        out_specs=pl.BlockSpec((tm, d), lambda i: (i, 0)),
        out_shape=jax.ShapeDtypeStruct((t, d), F32),
        scratch_shapes=[pltpu.VMEM((tm + POOL_HALO, d), F32),
                        pltpu.VMEM((ng, POOL_GROUP, POOL_GROUP), BF16)],
        compiler_params=_params("arbitrary"),
        name="pool_mixer",
    )(x, x, g3, pool_w, sc3)


def _ffn_kernel(start_ref, expert_ref, nsub_ref, nused_ref,
                x_hbm, g_ref, wg_ref, wu_ref, wd_ref, o_hbm,
                xs_ref, acc_ref, xbuf_ref, in_sem, out_sem, *,
                residual):
    s = pl.program_id(0)
    f = pl.program_id(1)
    nf = pl.num_programs(1)
    nsub = nsub_ref[s]
    row0 = start_ref[s] * FFN_SUB

    def in_copy(c, slot):
        return pltpu.make_async_copy(x_hbm.at[pl.ds(row0 + c * FFN_SUB, FFN_SUB)],
                                     xbuf_ref.at[slot], in_sem.at[slot])

    def out_copy(c):
        r = pl.multiple_of(c * FFN_SUB, FFN_SUB)
        return pltpu.make_async_copy(acc_ref.at[pl.ds(r, FFN_SUB)],
                                     o_hbm.at[pl.ds(row0 + c * FFN_SUB, FFN_SUB)], out_sem.at[0])

    @pl.when((nsub > 0) & (f == 0))
    def _():
        in_copy(0, 0).start()

        def load(c, carry):
            slot = c % 2

            @pl.when(c + 1 < nsub)
            def _():
                in_copy(c + 1, 1 - slot).start()

            in_copy(c, slot).wait()
            xr = xbuf_ref[slot]
            r = pl.multiple_of(c * FFN_SUB, FFN_SUB)
            xs_ref[pl.ds(r, FFN_SUB), :] = _rms(xr, g_ref[...], NORM_EPS).astype(BF16)
            acc_ref[pl.ds(r, FFN_SUB), :] = xr if residual else jnp.zeros_like(xr)
            return carry

        lax.fori_loop(0, nsub, load, 0)

    @pl.when(nsub > 0)
    def _():
        def rows_step(first_sub, n_sub):
            rows = pl.ds(pl.multiple_of(first_sub * FFN_SUB, FFN_SUB), n_sub * FFN_SUB)
            xc = xs_ref[rows, :]
            gate = jnp.dot(xc, wg_ref[...].astype(BF16), preferred_element_type=F32)
            up = jnp.dot(xc, wu_ref[...].astype(BF16), preferred_element_type=F32)
            act = (gate * jax.nn.sigmoid(gate) * up).astype(BF16)
            acc_ref[rows, :] += jnp.dot(act, wd_ref[...].astype(BF16), preferred_element_type=F32)

        def chain(cq, carry):
            rows_step(FFN_CHAIN * cq, FFN_CHAIN)
            return carry

        rest = nsub % FFN_CHAIN
        merge = (rest == 1) & (nsub > FFN_CHAIN)
        n_full = nsub // FFN_CHAIN - merge.astype(jnp.int32)
        lax.fori_loop(0, n_full, chain, 0)
        last = jnp.where(merge, FFN_CHAIN + 1, rest)
        for size in range(1, FFN_CHAIN + 2):
            if size == FFN_CHAIN:
                continue

            @pl.when(last == size)
            def _(size=size):
                rows_step(n_full * FFN_CHAIN, size)

    @pl.when((nsub > 0) & (f == nf - 1))
    def _():
        def start(c, carry):
            out_copy(c).start()
            return carry

        def wait(c, carry):
            out_copy(c).wait()
            return carry

        lax.fori_loop(0, nsub, start, 0)
        lax.fori_loop(0, nsub, wait, 0)

    @pl.when((s == pl.num_programs(0) - 1) & (f == nf - 1))
    def _():
        xbuf_ref[0] = jnp.zeros(xbuf_ref.shape[1:], F32)

        def zero_copy(c):
            return pltpu.make_async_copy(xbuf_ref.at[0], o_hbm.at[pl.ds(c * FFN_SUB, FFN_SUB)],
                                         out_sem.at[0])

        def start(c, carry):
            zero_copy(c).start()
            return carry

        def wait(c, carry):
            zero_copy(c).wait()
            return carry

        lax.fori_loop(nused_ref[1], o_hbm.shape[0] // FFN_SUB, start, 0)
        lax.fori_loop(nused_ref[1], o_hbm.shape[0] // FFN_SUB, wait, 0)


def _ffn(x_rows, gains, layer, w_gate, w_up, w_down, sb_start, sb_expert, sb_nsub, n_used, out_rows,
         residual, max_nsub):
    d = x_rows.shape[1]
    dff = w_gate.shape[2]
    nsb = sb_start.shape[0]
    nf = dff // FFN_TF
    rows = FFN_SUB * max_nsub
    g3 = gains.reshape(gains.shape[0], 1, d)

    def w_in(s, f, st, ex, ns, nu):
        return (ex[s], 0, jnp.where(s < nu[0], f, nf - 1))

    def w_out(s, f, st, ex, ns, nu):
        return (ex[s], jnp.where(s < nu[0], f, nf - 1), 0)

    grid_spec = pltpu.PrefetchScalarGridSpec(
        num_scalar_prefetch=4,
        grid=(n_used[0], nf),
        in_specs=[pl.BlockSpec(memory_space=pl.ANY),
                  pl.BlockSpec((None, 1, d), lambda s, f, *_: (layer, 0, 0)),
                  pl.BlockSpec((None, d, FFN_TF), w_in),
                  pl.BlockSpec((None, d, FFN_TF), w_in),
                  pl.BlockSpec((None, FFN_TF, d), w_out)],
        out_specs=pl.BlockSpec(memory_space=pl.ANY),
        scratch_shapes=[pltpu.VMEM((rows, d), BF16),
                        pltpu.VMEM((rows, d), F32),
                        pltpu.VMEM((2, FFN_SUB, d), F32),
                        pltpu.SemaphoreType.DMA((2,)),
                        pltpu.SemaphoreType.DMA((1,))],
    )
    kern = functools.partial(_ffn_kernel, residual=residual)
    return pl.pallas_call(
        kern,
        grid_spec=grid_spec,
        out_shape=jax.ShapeDtypeStruct((out_rows, d), F32),
        compiler_params=pltpu.CompilerParams(dimension_semantics=("arbitrary", "arbitrary"),
                                             vmem_limit_bytes=60 * 1024 * 1024),
        name="ffn_residual" if residual else "ffn_experts",
    )(sb_start, sb_expert, sb_nsub, n_used, x_rows, g3, w_gate, w_up, w_down)


def _dense_ffn(x, gains, layer, w_gate, w_up, w_down, m):
    t = x.shape[0]
    total_sub = t // FFN_SUB
    per_sb = 2 * FFN_CHAIN
    nsb = -(-total_sub // per_sb)
    start = np.arange(nsb, dtype=np.int32) * per_sb
    nsub = np.minimum(per_sb, total_sub - start).astype(np.int32)
    return _ffn(x, gains, layer, w_gate, w_up, w_down,
                jnp.asarray(start), jnp.full((nsb,), m, jnp.int32), jnp.asarray(nsub),
                jnp.asarray([nsb, total_sub], jnp.int32), t, residual=True, max_nsub=per_sb)


def _router_kernel(x_ref, g_ref, wr_ref, idx_ref, gate_ref, cnt_ref, carry_ref, *, tm):
    @pl.when(pl.program_id(0) == 0)
    def _():
        carry_ref[...] = jnp.zeros(carry_ref.shape, F32)

    h = _rms(x_ref[...], g_ref[...], NORM_EPS)
    logit = [jnp.sum(h * wr_ref[e:e + 1, :], axis=-1, keepdims=True) for e in range(N_EXPERTS)]

    def top(vals):
        best = vals[0]
        for v in vals[1:]:
            best = jnp.maximum(best, v)
        idx = jnp.full(best.shape, N_EXPERTS, jnp.int32)
        for e in reversed(range(N_EXPERTS)):
            idx = jnp.where(vals[e] == best, e, idx)
        return best, idx

    m1, i1 = top(logit)
    m2, i2 = top([jnp.where(i1 == e, -jnp.inf, logit[e]) for e in range(N_EXPERTS)])
    d = jnp.exp(m2 - m1)
    g1 = 1.0 / (1.0 + d)
    g2 = d / (1.0 + d)

    lane = lax.broadcasted_iota(jnp.int32, (tm, LANES), 1)
    onehot = ((lane == i1) | (lane == i2)).astype(BF16)
    r = lax.broadcasted_iota(jnp.int32, (tm, tm), 0)
    c = lax.broadcasted_iota(jnp.int32, (tm, tm), 1)
    before = (c < r).astype(BF16)
    rank = jnp.dot(before, onehot, preferred_element_type=F32) + carry_ref[...]
    r1 = jnp.sum(jnp.where(lane == i1, rank, 0.0), axis=-1, keepdims=True).astype(jnp.int32)
    r2 = jnp.sum(jnp.where(lane == i2, rank, 0.0), axis=-1, keepdims=True).astype(jnp.int32)
    carry_ref[...] += jnp.sum(onehot.astype(F32), axis=0, keepdims=True)
    cnt_ref[...] = carry_ref[...]

    l8 = lax.broadcasted_iota(jnp.int32, (tm, 8), 1)
    idx_ref[...] = jnp.where(l8 == 0, i1, jnp.where(l8 == 1, i2, jnp.where(l8 == 2, r1, r2)))
    gate_ref[...] = jnp.where(l8 == 0, g1, g2)


def _router(x, gains, layer, router_t, m):
    t, d = x.shape
    tm = ROUTER_TM
    g3 = gains.reshape(gains.shape[0], 1, d)
    kern = functools.partial(_router_kernel, tm=tm)
    return pl.pallas_call(
        kern,
        grid=(t // tm,),
        in_specs=[pl.BlockSpec((tm, d), lambda i: (i, 0)),
                  pl.BlockSpec((None, 1, d), lambda i: (layer, 0, 0)),
                  pl.BlockSpec((None, N_EXPERTS, d), lambda i: (m, 0, 0))],
        out_specs=[pl.BlockSpec((tm, 8), lambda i: (i, 0)),
                   pl.BlockSpec((tm, 8), lambda i: (i, 0)),
                   pl.BlockSpec((1, LANES), lambda i: (0, 0))],
        out_shape=[jax.ShapeDtypeStruct((t, 8), jnp.int32),
                   jax.ShapeDtypeStruct((t, 8), F32),
                   jax.ShapeDtypeStruct((1, LANES), F32)],
        scratch_shapes=[pltpu.VMEM((1, LANES), F32)],
        compiler_params=_params("arbitrary"),
        name="router",
    )(x, g3, router_t)


def _scatter_kernel(d0_ref, d1_ref, pad_ref, x_ref, o_hbm, zero_ref, sem, *, ct):
    base = pl.program_id(0) * ct

    @pl.when(pl.program_id(0) == 0)
    def _():
        zero_ref[...] = jnp.zeros(zero_ref.shape, F32)
        total_sub = o_hbm.shape[0] // FFN_SUB

        def fill(row, nrows):
            return pltpu.make_async_copy(zero_ref.at[pl.ds(0, nrows)],
                                         o_hbm.at[pl.ds(pl.multiple_of(row, 8), nrows)], sem.at[1])

        def window(e):
            return fill(pad_ref[e] // 8 * 8, zero_ref.shape[0])

        def tail_start(c, carry):
            fill(c * FFN_SUB, FFN_SUB).start()
            return carry

        def tail_wait(c, carry):
            fill(c * FFN_SUB, FFN_SUB).wait()
            return carry

        for e in range(N_EXPERTS):
            window(e).start()
            window(e).wait()
        lax.fori_loop(pad_ref[N_EXPERTS], total_sub, tail_start, 0)
        lax.fori_loop(pad_ref[N_EXPERTS], total_sub, tail_wait, 0)

    def issue(t, carry):
        tok = base + t
        pltpu.make_async_copy(x_ref.at[pl.ds(t, 1)], o_hbm.at[pl.ds(d0_ref[tok], 1)], sem.at[0]).start()
        pltpu.make_async_copy(x_ref.at[pl.ds(t, 1)], o_hbm.at[pl.ds(d1_ref[tok], 1)],
                              sem.at[0]).start(priority=1)
        return carry

    def wait(t, carry):
        pltpu.make_async_copy(x_ref.at[pl.ds(0, 1)], o_hbm.at[pl.ds(0, 1)], sem.at[0]).wait()
        return carry

    lax.fori_loop(0, ct, issue, 0)
    lax.fori_loop(0, 2 * ct, wait, 0)


def _scatter_rows(x, d0, d1, pad, n_rows):
    t, d = x.shape
    ct = SCATTER_CT
    kern = functools.partial(_scatter_kernel, ct=ct)
    grid_spec = pltpu.PrefetchScalarGridSpec(
        num_scalar_prefetch=3,
        grid=(t // ct,),
        in_specs=[pl.BlockSpec((ct, d), lambda i, *_: (i, 0))],
        out_specs=pl.BlockSpec(memory_space=pl.ANY),
        scratch_shapes=[pltpu.VMEM((FFN_SUB + 8, d), F32), pltpu.SemaphoreType.DMA((2,))],
    )
    return pl.pallas_call(
        kern,
        grid_spec=grid_spec,
        out_shape=jax.ShapeDtypeStruct((n_rows + 2 * FFN_SUB, d), F32),
        compiler_params=_params("arbitrary"),
        name="scatter_rows",
    )(d0, d1, pad, x)


def _combine_kernel(d0_ref, d1_ref, x_ref, gate_ref, g_ref, y_hbm, *rest, ct, with_x):
    if with_x:
        o_ref, h_ref, buf_ref, sem = rest
    else:
        h_ref, buf_ref, sem = rest
    step = pl.program_id(0)
    nsteps = pl.num_programs(0)

    def row_copy(d_ref, k, slot, tok, t):
        return pltpu.make_async_copy(y_hbm.at[pl.ds(d_ref[tok], 1)],
                                     buf_ref.at[slot, k, pl.ds(t, 1)], sem.at[slot])

    def issue_all(for_step, slot):
        def issue(t, carry):
            tok = for_step * ct + t
            row_copy(d0_ref, 0, slot, tok, t).start()
            row_copy(d1_ref, 1, slot, tok, t).start(priority=1)
            return carry

        lax.fori_loop(0, ct, issue, 0)

    slot = step % 2

    @pl.when(step == 0)
    def _():
        issue_all(0, 0)

    @pl.when(step + 1 < nsteps)
    def _():
        issue_all(step + 1, 1 - slot)

    def wait(t, carry):
        row_copy(d0_ref, 0, slot, step * ct, 0).wait()
        row_copy(d1_ref, 1, slot, step * ct, 0).wait()
        return carry

    lax.fori_loop(0, ct, wait, 0)
    gts = gate_ref[...]
    y = x_ref[...] + (buf_ref[slot, 0] * gts[:, 0:1] + buf_ref[slot, 1] * gts[:, 1:2])
    if with_x:
        o_ref[...] = y
    h_ref[...] = _rms(y, g_ref[...], NORM_EPS).astype(h_ref.dtype)


def _combine(x, y_rows, gates, d0, d1, next_gains, next_layer, h_dtype, with_x):
    t, d = x.shape
    ct = COMBINE_CT
    kern = functools.partial(_combine_kernel, ct=ct, with_x=with_x)
    g3 = next_gains.reshape(next_gains.shape[0], 1, d)
    row_spec = pl.BlockSpec((ct, d), lambda i, *_: (i, 0))
    grid_spec = pltpu.PrefetchScalarGridSpec(
        num_scalar_prefetch=2,
        grid=(t // ct,),
        in_specs=[row_spec,
                  pl.BlockSpec((ct, 8), lambda i, *_: (i, 0)),
                  pl.BlockSpec((None, 1, d), lambda i, *_: (next_layer, 0, 0)),
                  pl.BlockSpec(memory_space=pl.ANY)],
        out_specs=[row_spec, row_spec] if with_x else [row_spec],
        scratch_shapes=[pltpu.VMEM((2, 2, ct, d), F32), pltpu.SemaphoreType.DMA((2,))],
    )
    h_shape = jax.ShapeDtypeStruct((t, d), h_dtype)
    return pl.pallas_call(
        kern,
        grid_spec=grid_spec,
        out_shape=[jax.ShapeDtypeStruct((t, d), F32), h_shape] if with_x else [h_shape],
        compiler_params=_params("arbitrary"),
        name="moe_combine",
    )(d0, d1, x, gates, g3, y_rows)


def _moe_ffn(x, gains, layer, router_w, w_gate, w_up, w_down, m, next_gains, next_layer, h_dtype,
             with_x):
    t, d = x.shape
    router_t = jnp.swapaxes(router_w, 1, 2)
    idx, gates, counts = _router(x, gains, layer, router_t, m)

    cnt = counts[0, :N_EXPERTS].astype(jnp.int32)
    nsub_e = (cnt + FFN_SUB - 1) // FFN_SUB
    start_e = jnp.cumsum(nsub_e) - nsub_e
    row_start = start_e * FFN_SUB
    d0 = row_start[idx[:, 0]] + idx[:, 2]
    d1 = row_start[idx[:, 1]] + idx[:, 3]

    max_sub = t * 2 // FFN_SUB + N_EXPERTS
    n_rows = max_sub * FFN_SUB
    nsb = -(-(t * 2 // FFN_SUB) // FFN_NSUB) + N_EXPERTS
    sbs_e = (nsub_e + FFN_NSUB - 1) // FFN_NSUB
    cum = jnp.cumsum(sbs_e)
    n_used = cum[-1]
    slot = jnp.arange(nsb, dtype=jnp.int32)
    e_of = jnp.minimum(jnp.sum((slot[:, None] >= cum[None, :]).astype(jnp.int32), axis=1), N_EXPERTS - 1)
    k_of = slot - (cum - sbs_e)[e_of]
    used = slot < n_used
    last_e = e_of[jnp.maximum(n_used - 1, 0)]
    sb_start = jnp.where(used, start_e[e_of] + k_of * FFN_NSUB, 0).astype(jnp.int32)
    sb_nsub = jnp.where(used, jnp.clip(nsub_e[e_of] - k_of * FFN_NSUB, 0, FFN_NSUB), 0).astype(jnp.int32)
    sb_expert = (m * N_EXPERTS + jnp.where(used, e_of, last_e)).astype(jnp.int32)

    flat = lambda w: w.reshape((-1,) + w.shape[2:])
    pad = jnp.concatenate([row_start + cnt, jnp.sum(nsub_e)[None]]).astype(jnp.int32)
    x_rows = _scatter_rows(x, d0, d1, pad, n_rows)
    y_rows = _ffn(x_rows, gains, layer, flat(w_gate), flat(w_up), flat(w_down),
                  sb_start, sb_expert, sb_nsub,
                  jnp.stack([n_used, jnp.sum(nsub_e)]).astype(jnp.int32), n_rows, residual=False,
                  max_nsub=FFN_NSUB)
    return _combine(x, y_rows, gates, d0, d1, next_gains, next_layer, h_dtype, with_x)


def kernel(x, positions, norm1, norm2, final_norm, diff_w_qkv, diff_lambda_q1, diff_lambda_k1, diff_lambda_q2, diff_lambda_k2, diff_subln, diff_w_o, pool_w, pool_scale, swa_w_qkv, swa_b_qkv, swa_sinks, swa_w_o, swa_b_o, ffn_w_gate, ffn_w_up, ffn_w_down, moe_router, moe_w_gate, moe_w_up, moe_w_down):
    b, s, d = x.shape
    assert b == 1 and d == D_MODEL
    xs = x.reshape(s, d)
    pos_col = positions.reshape(s, 1)
    cos_a, sin_a = _rope_tables(pos_col, DIFF_DH)
    cos_c, sin_c = _rope_tables(pos_col, SWA_DH)
    zero_bias_qkv = jnp.zeros((1, diff_w_qkv.shape[2]), F32)
    zero_bias_o = jnp.zeros((1, d), F32)

    h = None
    for i in range(N_LAYERS):
        kind, j = i % 3, i // 3
        if kind != 1 and h is None:
            h = _norm(xs, norm1, i, BF16)
        if kind == 0:
            lambda_init = 0.8 - 0.6 * math.exp(-0.3 * i)
            qkv = _mm_qkv(h, diff_w_qkv, j, zero_bias_qkv, cos_a, sin_a, half=DIFF_DH // 2,
                          scale=DIFF_DH ** -0.5 * math.log2(math.e), q_cols=2 * DIFF_HEADS * DIFF_DH,
                          k_cols=2 * DIFF_HEADS * DIFF_DH)
            att = _diff_attention(qkv, diff_lambda_q1, diff_lambda_k1, diff_lambda_q2, diff_lambda_k2,
                                  diff_subln, j, lambda_init)
            xs = _mm_residual(att, diff_w_o, j, zero_bias_o, xs)
        elif kind == 1:
            xs = _pool_mixer(xs, norm1, i, pool_w, pool_scale, j)
        else:
            qkv = _mm_qkv(h, swa_w_qkv, j, swa_b_qkv[j][None, :], cos_c, sin_c, half=SWA_DH // 2,
                          scale=SWA_DH ** -0.5, q_cols=SWA_Q_HEADS * SWA_DH,
                          k_cols=SWA_KV_HEADS * SWA_DH)
            att = _swa_attention(qkv, swa_sinks[j])
            xs = _mm_residual(att, swa_w_o, j, swa_b_o[j][None, :], xs)
        m = i // 2
        h = None
        if i % 2 == 0:
            xs = _dense_ffn(xs, norm2, i, ffn_w_gate, ffn_w_up, ffn_w_down, m)
        elif i + 1 < N_LAYERS:
            xs, h = _moe_ffn(xs, norm2, i, moe_router, moe_w_gate, moe_w_up, moe_w_down, m,
                             norm1, i + 1, BF16, True)
        else:
            (out,) = _moe_ffn(xs, norm2, i, moe_router, moe_w_gate, moe_w_up, moe_w_down, m,
                              final_norm[None, :], 0, F32, False)
            return out.reshape(b, s, d)
    return _norm(xs, final_norm[None, :], 0, F32).reshape(b, s, d)
```
